```python
import math
import jax, jax.numpy as jnp
from jax import lax
import numpy as np

D_MODEL = 1024
BATCH = 2
SEQ = 16384
DEPTH = 2

CHUNK = 64
N_EVEN = (DEPTH + 1) // 2
N_ODD = DEPTH // 2
RMS_EPS = 1e-6
ROPE_THETA = 10000.0

DN_HEADS = 4
DN_DK = 128
DN_DV = 128
DN_CONV = 4
DN_QK = DN_HEADS * DN_DK
DN_V = DN_HEADS * DN_DV
DIFF_HEADS = 4
DIFF_HD = 64
DIFF_QK = DIFF_HEADS * 2 * DIFF_HD
DIFF_V = DIFF_HEADS * 2 * DIFF_HD
Q_BLOCK = 128
IN0_SPLITS = (3 * DN_QK if DN_QK == DN_V else 2 * DN_QK + DN_V, DN_HEADS, DN_HEADS, DN_V, DIFF_QK, DIFF_QK, DIFF_V)
IN0_WIDTH = sum(IN0_SPLITS)
MIX0_OUT = DN_V + DIFF_V
CA_HEADS = 16
CA_HD = D_MODEL // CA_HEADS
CA_LEFT_CHUNKS = 8
REL_CLIP = 256
D_FF = 2816
FFN_CONV = 3

kernel_name = "hybrid_deltanet_diffattn_chunkband_convffn"


def rmsnorm(x, g):
    xf = x.astype(jnp.float32)
    y = xf * lax.rsqrt(jnp.mean(xf * xf, axis=-1, keepdims=True) + RMS_EPS)
    return (y * g.astype(jnp.float32)).astype(x.dtype)


def l2norm(x):
    return x * lax.rsqrt(jnp.sum(x * x, axis=-1, keepdims=True) + RMS_EPS)


def split_cols(t, sizes):
    return jnp.split(t, [int(v) for v in np.cumsum(sizes)[:-1]], axis=-1)


def causal_dwconv(x, w):
    K, C = w.shape
    return lax.conv_general_dilated(
        x, w[:, None, :].astype(x.dtype), window_strides=(1,), padding=[(K - 1, 0)],
        dimension_numbers=("NWC", "WIO", "NWC"), feature_group_count=C)


def rope_tables(S, dim):
    inv = 1.0 / (ROPE_THETA ** (jnp.arange(0, dim, 2, dtype=jnp.float32) / dim))
    ang = jnp.arange(S, dtype=jnp.float32)[:, None] * inv[None, :]
    return jnp.cos(ang), jnp.sin(ang)


def apply_rope(x, cos, sin):
    half = x.shape[-1] // 2
    x1 = x[..., :half].astype(jnp.float32)
    x2 = x[..., half:].astype(jnp.float32)
    return jnp.concatenate([x1 * cos - x2 * sin, x2 * cos + x1 * sin], axis=-1).astype(x.dtype)


def gated_deltanet(q, k, v, a, b, A_log, dt_bias):
    out_dtype = v.dtype
    Bsz, S, H, dk = q.shape
    dv = v.shape[-1]
    n = S // CHUNK
    q = l2norm(q.astype(jnp.float32)) * (dk ** -0.5)
    k = l2norm(k.astype(jnp.float32))
    v = v.astype(jnp.float32)
    beta = jax.nn.sigmoid(b.astype(jnp.float32))
    g = -jnp.exp(A_log.astype(jnp.float32)) * jax.nn.softplus(a.astype(jnp.float32) + dt_bias.astype(jnp.float32))

    def chunks(t):
        return t.reshape(Bsz, n, CHUNK, H, -1).transpose(0, 3, 1, 2, 4)

    qc, kc, vc = chunks(q), chunks(k), chunks(v)
    bc = beta.reshape(Bsz, n, CHUNK, H).transpose(0, 3, 1, 2)
    gcum = jnp.cumsum(g.reshape(Bsz, n, CHUNK, H).transpose(0, 3, 1, 2), axis=-1)
    tri = jnp.tril(jnp.ones((CHUNK, CHUNK), bool))
    strict = jnp.tril(jnp.ones((CHUNK, CHUNK), bool), -1)
    gdiff = gcum[..., :, None] - gcum[..., None, :]
    decay = jnp.where(tri, jnp.exp(jnp.where(tri, gdiff, 0.0)), 0.0)
    kk = jnp.einsum("bhnid,bhnjd->bhnij", kc, kc)
    Lmat = jnp.where(strict, bc[..., :, None] * kk * decay, 0.0) + jnp.eye(CHUNK, dtype=jnp.float32)
    rhs = jnp.concatenate([vc * bc[..., None], kc * (bc * jnp.exp(gcum))[..., None]], axis=-1)
    sol = lax.linalg.triangular_solve(Lmat, rhs, left_side=True, lower=True, unit_diagonal=True)
    u, w = sol[..., :dv], sol[..., dv:]
    qk = jnp.where(tri, jnp.einsum("bhnid,bhnjd->bhnij", qc, kc) * decay, 0.0)
    q_dec = qc * jnp.exp(gcum)[..., None]
    k_dec = kc * jnp.exp(gcum[..., -1:] - gcum)[..., None]
    g_last = jnp.exp(gcum[..., -1])

    def step(state, xs):
        u_i, w_i, qk_i, qd_i, kd_i, gl_i = xs
        v_new = u_i - jnp.einsum("bhck,bhkv->bhcv", w_i, state)
        o = jnp.einsum("bhck,bhkv->bhcv", qd_i, state) + jnp.einsum("bhij,bhjv->bhiv", qk_i, v_new)
        state = state * gl_i[..., None, None] + jnp.einsum("bhck,bhcv->bhkv", kd_i, v_new)
        return state, o

    xs = tuple(jnp.moveaxis(t, 2, 0) for t in (u, w, qk, q_dec, k_dec, g_last))
    state0 = jnp.zeros((Bsz, H, dk, dv), jnp.float32)
    _, o = lax.scan(step, state0, xs)
    return o.transpose(1, 0, 3, 2, 4).reshape(Bsz, S, H, dv).astype(out_dtype)


def diff_attention(q, k, v, lam):
    Bsz, S, H, _, Dh = q.shape
    nb = S // Q_BLOCK
    scale = Dh ** -0.5
    qb = q.reshape(Bsz, nb, Q_BLOCK, H, 2, Dh).transpose(1, 0, 2, 3, 4, 5)
    key_chunk = jnp.arange(S) // CHUNK

    def block(args):
        q_i, i = args
        s = jnp.einsum("bqhmd,bkhmd->bhmqk", q_i, k, preferred_element_type=jnp.float32) * scale
        q_chunk = (i * Q_BLOCK + jnp.arange(Q_BLOCK)) // CHUNK
        mask = key_chunk[None, :] <= q_chunk[:, None]
        p = jax.nn.softmax(jnp.where(mask, s, -jnp.inf), axis=-1)
        attn = p[:, :, 0] - lam * p[:, :, 1]
        return jnp.einsum("bhqk,bkhd->bqhd", attn, v.astype(jnp.float32))

    o = lax.map(block, (qb, jnp.arange(nb)))
    return o.transpose(1, 0, 2, 3, 4).reshape(Bsz, S, H, 2 * Dh).astype(v.dtype)


def chunk_band_attention(q, k, v, rel_bias):
    Bsz, S, H, Dh = q.shape
    n = S // CHUNK
    band = (CA_LEFT_CHUNKS + 1) * CHUNK
    pad = CA_LEFT_CHUNKS * CHUNK
    kp = jnp.pad(k, ((0, 0), (pad, 0), (0, 0), (0, 0)))
    vp = jnp.pad(v, ((0, 0), (pad, 0), (0, 0), (0, 0)))
    qi = jnp.arange(CHUNK)
    kj = jnp.arange(band)
    rel = kj[None, :] - pad - qi[:, None]
    bias = rel_bias.astype(jnp.float32)[:, jnp.clip(rel, -REL_CLIP, REL_CLIP) + REL_CLIP]
    qc = q.reshape(Bsz, n, CHUNK, H, Dh).transpose(1, 0, 2, 3, 4)
    scale = Dh ** -0.5

    def one(args):
        q_c, c = args
        k_b = lax.dynamic_slice_in_dim(kp, c * CHUNK, band, axis=1)
        v_b = lax.dynamic_slice_in_dim(vp, c * CHUNK, band, axis=1)
        s = jnp.einsum("bqhd,bkhd->bhqk", q_c, k_b, preferred_element_type=jnp.float32) * scale + bias
        valid = kj >= pad - c * CHUNK
        p = jax.nn.softmax(jnp.where(valid, s, -jnp.inf), axis=-1)
        return jnp.einsum("bhqk,bkhd->bqhd", p, v_b.astype(jnp.float32))

    o = lax.map(one, (qc, jnp.arange(n)))
    return o.transpose(1, 0, 2, 3, 4).reshape(Bsz, S, H * Dh).astype(v.dtype)


def conv_gated_mlp(h, w_in, conv_w, conv_b, w_out):
    gate, up = split_cols(h @ w_in, (D_FF, D_FF))
    gate = causal_dwconv(gate, conv_w) + conv_b
    return (jax.nn.silu(gate) * up) @ w_out


def setup_inputs(seed: int = 0) -> dict:
    key = jax.random.key(seed)
    ks = iter(jax.random.split(key, 40))

    def nrm(shape, scale):
        return jax.random.normal(next(ks), shape, jnp.float32) * scale

    def gain(shape):
        return 1.0 + nrm(shape, 0.02)

    dt = jnp.exp(jax.random.uniform(next(ks), (N_EVEN, DN_HEADS), jnp.float32, math.log(1e-3), math.log(1e-1)))
    return {
        "x": nrm((BATCH, SEQ, D_MODEL), 1.0),
        "even_norm_mix": gain((N_EVEN, D_MODEL)),
        "even_w_in": nrm((N_EVEN, D_MODEL, IN0_WIDTH), D_MODEL ** -0.5),
        "even_dn_conv": nrm((N_EVEN, DN_CONV, 2 * DN_QK + DN_V), DN_CONV ** -0.5),
        "even_dn_A_log": jnp.log(jax.random.uniform(next(ks), (N_EVEN, DN_HEADS), jnp.float32, 1.0, 16.0)),
        "even_dn_dt_bias": dt + jnp.log(-jnp.expm1(-dt)),
        "even_dn_norm": gain((N_EVEN, DN_DV)),
        "even_diff_lambda": nrm((N_EVEN, 4, DIFF_HD), 0.1),
        "even_diff_subln": gain((N_EVEN, 2 * DIFF_HD)),
        "even_w_out": nrm((N_EVEN, MIX0_OUT, D_MODEL), MIX0_OUT ** -0.5),
        "odd_norm_mix": gain((N_ODD, D_MODEL)),
        "odd_w_qkv": nrm((N_ODD, D_MODEL, 3 * CA_HEADS * CA_HD), D_MODEL ** -0.5),
        "odd_rel_bias": nrm((N_ODD, CA_HEADS, 2 * REL_CLIP + 1), 0.2),
        "odd_w_out": nrm((N_ODD, CA_HEADS * CA_HD, D_MODEL), (CA_HEADS * CA_HD) ** -0.5),
        "ffn_norm": gain((DEPTH, D_MODEL)),
        "ffn_w_in": nrm((DEPTH, D_MODEL, 2 * D_FF), D_MODEL ** -0.5),
        "ffn_conv_w": nrm((DEPTH, FFN_CONV, D_FF), FFN_CONV ** -0.5),
        "ffn_conv_b": nrm((DEPTH, D_FF), 0.01),
        "ffn_w_out": nrm((DEPTH, D_FF, D_MODEL), D_FF ** -0.5),
        "final_norm": gain((D_MODEL,)),
    }


def reference(x, even_norm_mix, even_w_in, even_dn_conv, even_dn_A_log, even_dn_dt_bias,
              even_dn_norm, even_diff_lambda, even_diff_subln, even_w_out,
              odd_norm_mix, odd_w_qkv, odd_rel_bias, odd_w_out,
              ffn_norm, ffn_w_in, ffn_conv_w, ffn_conv_b, ffn_w_out, final_norm):
    Bsz, S, _ = x.shape
    cos, sin = rope_tables(S, DIFF_HD)
    cos_b, sin_b = cos[:, None, None, :], sin[:, None, None, :]
    for i in range(DEPTH):
        if i % 2 == 0:
            e = i // 2
            h = rmsnorm(x, even_norm_mix[e])
            dn_qkv, dn_a, dn_b, dn_gate, fq, fk, fv = split_cols(h @ even_w_in[e], IN0_SPLITS)
            dn_qkv = jax.nn.silu(causal_dwconv(dn_qkv, even_dn_conv[e]))
            dq, dk, dvv = split_cols(dn_qkv, (DN_QK, DN_QK, DN_V))
            o_a = gated_deltanet(dq.reshape(Bsz, S, DN_HEADS, DN_DK), dk.reshape(Bsz, S, DN_HEADS, DN_DK),
                                 dvv.reshape(Bsz, S, DN_HEADS, DN_DV), dn_a, dn_b,
                                 even_dn_A_log[e], even_dn_dt_bias[e])
            o_a = rmsnorm(o_a, even_dn_norm[e]) * jax.nn.silu(dn_gate.reshape(Bsz, S, DN_HEADS, DN_DV))
            o_a = o_a.reshape(Bsz, S, DN_V)
            lam_init = 0.8 - 0.6 * math.exp(-0.3 * i)
            lp = even_diff_lambda[e].astype(jnp.float32)
            lam = jnp.exp(jnp.sum(lp[0] * lp[1])) - jnp.exp(jnp.sum(lp[2] * lp[3])) + lam_init
            qd = apply_rope(fq.reshape(Bsz, S, DIFF_HEADS, 2, DIFF_HD), cos_b, sin_b)
            kd = apply_rope(fk.reshape(Bsz, S, DIFF_HEADS, 2, DIFF_HD), cos_b, sin_b)
            o_b = diff_attention(qd, kd, fv.reshape(Bsz, S, DIFF_HEADS, 2 * DIFF_HD), lam)
            o_b = (rmsnorm(o_b, even_diff_subln[e]) * (1.0 - lam_init)).reshape(Bsz, S, DIFF_V)
            x = x + jnp.concatenate([o_a, o_b], axis=-1) @ even_w_out[e]
        else:
            o_i = i // 2
            h = rmsnorm(x, odd_norm_mix[o_i])
            cq, ck, cv = split_cols(h @ odd_w_qkv[o_i], (D_MODEL, D_MODEL, D_MODEL))
            shp = (Bsz, S, CA_HEADS, CA_HD)
            o_c = chunk_band_attention(cq.reshape(shp), ck.reshape(shp), cv.reshape(shp), odd_rel_bias[o_i])
            x = x + o_c @ odd_w_out[o_i]
        h = rmsnorm(x, ffn_norm[i])
        x = x + conv_gated_mlp(h, ffn_w_in[i], ffn_conv_w[i], ffn_conv_b[i], ffn_w_out[i])
    return rmsnorm(x, final_norm)
```

```python
import functools
import math

import jax
import jax.numpy as jnp
from jax import lax
from jax.experimental import pallas as pl
from jax.experimental.pallas import tpu as pltpu

F32 = jnp.float32
BF16 = jnp.bfloat16

D_MODEL = 1024
CHUNK = 64
RMS_EPS = 1e-6
ROPE_THETA = 10000.0
DN_HEADS = 4
DN_DK = 128
DN_QK = 512
DN_CONV = 4
DIFF_HEADS = 4
DIFF_HD = 64
CA_HEADS = 16
CA_HD = 64
CA_LEFT_CHUNKS = 8
REL_CLIP = 256
D_FF = 2816
MASKED = -1e30
VMEM_LIMIT = 56 * 1024 * 1024
LANES = 128
BAND_SUB = 128
BAND_WIN = BAND_SUB + CA_LEFT_CHUNKS * CHUNK


def _params(*sem):
    return pltpu.CompilerParams(dimension_semantics=sem, vmem_limit_bytes=VMEM_LIMIT)


def _rms(x, g):
    ms = jnp.mean(x * x, axis=-1, keepdims=True)
    return x * lax.rsqrt(ms + RMS_EPS) * g


def _silu(x):
    return x * jax.nn.sigmoid(x)


def _dot(a, b):
    return jnp.dot(a, b, preferred_element_type=F32)


def _dot_nt(a, b):
    return lax.dot_general(a, b, (((1,), (1,)), ((), ())), preferred_element_type=F32)


def _dot_tn(a, b):
    return lax.dot_general(a, b, (((0,), (0,)), ((), ())), preferred_element_type=F32)


def _split(x):
    hi = x.astype(BF16)
    lo = (x - hi.astype(F32)).astype(BF16)
    return hi, lo


def _dot_hp(a, b):
    ah, al = _split(a)
    bh, bl = _split(b)
    return _dot(ah, bh) + (_dot(ah, bl) + _dot(al, bh))


def _dot_exact_lhs(l_bf16, x):
    x1 = x.astype(BF16)
    r1 = x - x1.astype(F32)
    x2 = r1.astype(BF16)
    x3 = (r1 - x2.astype(F32)).astype(BF16)
    return _dot(l_bf16, x1) + (_dot(l_bf16, x2) + _dot(l_bf16, x3))


def _norm_matmul_kernel(x_ref, g_ref, w_ref, o_ref, xn_ref):
    @pl.when(pl.program_id(1) == 0)
    def _():
        xn_ref[...] = _rms(x_ref[...], g_ref[...]).astype(BF16)

    o_ref[...] = _dot(xn_ref[...], w_ref[...]).astype(o_ref.dtype)


def _norm_matmul(x, g, w, out_dtype, tm, tn):
    t, d = x.shape
    n = w.shape[1]
    return pl.pallas_call(
        _norm_matmul_kernel,
        grid=(t // tm, n // tn),
        in_specs=[pl.BlockSpec((tm, d), lambda i, j: (i, 0)),
                  pl.BlockSpec((1, d), lambda i, j: (0, 0)),
                  pl.BlockSpec((d, tn), lambda i, j: (0, j))],
        out_specs=pl.BlockSpec((tm, tn), lambda i, j: (i, j)),
        out_shape=jax.ShapeDtypeStruct((t, n), out_dtype),
        scratch_shapes=[pltpu.VMEM((tm, d), BF16)],
        compiler_params=_params("parallel", "arbitrary"),
        name="norm_matmul",
    )(x, g, w)


def _norm_matmul_hp_kernel(x_ref, g_ref, whi_ref, wlo_ref, o_ref):
    xn = _rms(x_ref[...], g_ref[...])
    hi, lo = _split(xn)
    o_ref[...] = _dot(hi, whi_ref[...]) + (_dot(hi, wlo_ref[...]) + _dot(lo, whi_ref[...]))


def _norm_matmul_hp(x, g, w_hi, w_lo, tm):
    t, d = x.shape
    n = w_hi.shape[1]
    return pl.pallas_call(
        _norm_matmul_hp_kernel,
        grid=(t // tm,),
        in_specs=[pl.BlockSpec((tm, d), lambda i: (i, 0)),
                  pl.BlockSpec((1, d), lambda i: (0, 0)),
                  pl.BlockSpec((d, n), lambda i: (0, 0)),
                  pl.BlockSpec((d, n), lambda i: (0, 0))],
        out_specs=pl.BlockSpec((tm, n), lambda i: (i, 0)),
        out_shape=jax.ShapeDtypeStruct((t, n), F32),
        compiler_params=_params("parallel"),
        name="norm_matmul_hp",
    )(x, g, w_hi, w_lo)


def _proj_res_kernel(*refs, n_in):
    a_refs = refs[:n_in]
    w_refs = refs[n_in:2 * n_in]
    res_ref, o_ref = refs[2 * n_in:]
    acc = res_ref[...]
    for a_ref, w_ref in zip(a_refs, w_refs):
        acc = acc + _dot(a_ref[...], w_ref[...])
    o_ref[...] = acc


def _proj_res(acts, ws, res, tm):
    t, d = res.shape
    n_in = len(acts)
    in_specs = ([pl.BlockSpec((tm, a.shape[1]), lambda i: (i, 0)) for a in acts]
                + [pl.BlockSpec(w.shape, lambda i: (0, 0)) for w in ws]
                + [pl.BlockSpec((tm, d), lambda i: (i, 0))])
    return pl.pallas_call(
        functools.partial(_proj_res_kernel, n_in=n_in),
        grid=(t // tm,),
        in_specs=in_specs,
        out_specs=pl.BlockSpec((tm, d), lambda i: (i, 0)),
        out_shape=jax.ShapeDtypeStruct((t, d), F32),
        compiler_params=_params("parallel"),
        name="proj_res",
    )(*acts, *ws, res)


def _ffn_kernel(x_ref, g_ref, wg_ref, wu_ref, cw_ref, wo_ref, fg_ref, o_ref,
                xn_ref, acc_ref, carry_ref, gs_ref, *, tm, tiles_per_batch, final_norm):
    i = pl.program_id(0)
    j = pl.program_id(1)

    @pl.when(j == 0)
    def _():
        xn_ref[...] = _rms(x_ref[...], g_ref[...]).astype(BF16)

    xn = xn_ref[...]
    gate = _dot(xn, wg_ref[...])
    up = _dot(xn, wu_ref[...])
    keep = (i % tiles_per_batch) != 0
    gs_ref[0:8, :] = jnp.where(keep, carry_ref[j], 0.0)
    gs_ref[8:, :] = gate
    carry_ref[j] = gate[tm - 8:, :]
    cw = cw_ref[...]
    y = (cw[2:3] * gate + cw[1:2] * gs_ref[pl.ds(7, tm), :]
         + cw[0:1] * gs_ref[pl.ds(6, tm), :] + cw[3:4])
    act = (_silu(y) * up).astype(BF16)
    part = _dot(act, wo_ref[...])

    @pl.when(j == 0)
    def _():
        acc_ref[...] = x_ref[...] + part

    @pl.when(j > 0)
    def _():
        acc_ref[...] += part

    @pl.when(j == pl.num_programs(1) - 1)
    def _():
        r = acc_ref[...]
        o_ref[...] = _rms(r, fg_ref[...]) if final_norm else r


def _ffn(x, g, w_in, cwb, w_out, final_g, seq, tm, tf, final_norm):
    t, d = x.shape
    nff = D_FF // tf
    kern = functools.partial(_ffn_kernel, tm=tm, tiles_per_batch=seq // tm, final_norm=final_norm)
    return pl.pallas_call(
        kern,
        grid=(t // tm, nff),
        in_specs=[pl.BlockSpec((tm, d), lambda i, j: (i, 0)),
                  pl.BlockSpec((1, d), lambda i, j: (0, 0)),
                  pl.BlockSpec((d, tf), lambda i, j: (0, j)),
                  pl.BlockSpec((d, tf), lambda i, j: (0, nff + j)),
                  pl.BlockSpec((8, tf), lambda i, j: (0, j)),
                  pl.BlockSpec((tf, d), lambda i, j: (j, 0)),
                  pl.BlockSpec((1, d), lambda i, j: (0, 0))],
        out_specs=pl.BlockSpec((tm, d), lambda i, j: (i, 0)),
        out_shape=jax.ShapeDtypeStruct((t, d), F32),
        scratch_shapes=[pltpu.VMEM((tm, d), BF16),
                        pltpu.VMEM((tm, d), F32),
                        pltpu.VMEM((nff, 8, tf), F32),
                        pltpu.VMEM((tm + 8, tf), F32)],
        compiler_params=_params("arbitrary", "arbitrary"),
        name="ffn",
    )(x, g, w_in, w_in, cwb, w_out, final_g)


def _rope_kernel(q_ref, k_ref, cos_ref, sin_ref, qa_ref, qb_ref, kr_ref):
    cos = cos_ref[...]
    sin = sin_ref[...]
    lane = lax.broadcasted_iota(jnp.int32, cos.shape, 1)
    first = (lane % DIFF_HD) < (DIFF_HD // 2)

    def rope(x):
        rot = jnp.where(first, pltpu.roll(x, LANES - DIFF_HD // 2, 1), pltpu.roll(x, DIFF_HD // 2, 1))
        return x * cos + rot * sin

    q = rope(q_ref[...].astype(F32)) * (DIFF_HD ** -0.5)
    k = rope(k_ref[...].astype(F32))
    lo = lane < DIFF_HD
    qa_ref[...] = jnp.where(lo, q, 0.0).astype(BF16)
    qb_ref[...] = jnp.where(lo, 0.0, q).astype(BF16)
    kr_ref[...] = k.astype(BF16)


def _rope(p0, cos_t, sin_t, seq, tm, q_blk, k_blk):
    t = p0.shape[0]
    w = DIFF_HEADS * LANES
    out = jax.ShapeDtypeStruct((t, w), BF16)
    spt = seq // tm
    return pl.pallas_call(
        _rope_kernel,
        grid=(t // tm, DIFF_HEADS),
        in_specs=[pl.BlockSpec((tm, LANES), lambda i, h: (i, q_blk + h)),
                  pl.BlockSpec((tm, LANES), lambda i, h: (i, k_blk + h)),
                  pl.BlockSpec((tm, LANES), lambda i, h: (i % spt, 0)),
                  pl.BlockSpec((tm, LANES), lambda i, h: (i % spt, 0))],
        out_specs=[pl.BlockSpec((tm, LANES), lambda i, h: (i, h))] * 3,
        out_shape=[out, out, out],
        compiler_params=_params("parallel", "parallel"),
        name="rope",
    )(p0, p0, cos_t, sin_t)


def _diff_attn_kernel(qa_ref, qb_ref, k_ref, v_ref, lam_ref, gain_ref, o_ref,
                      m_ref, l_ref, acc_ref, *, tq):
    qi = pl.program_id(2)
    m_ref[...] = jnp.full(m_ref.shape, MASKED, F32)
    l_ref[...] = jnp.zeros(l_ref.shape, F32)
    acc_ref[...] = jnp.zeros(acc_ref.shape, F32)
    qs = (qa_ref[...], qb_ref[...])

    def step(k, v, mask):
        for mp in range(2):
            s = _dot_nt(qs[mp], k)
            if mask is not None:
                s = jnp.where(mask, s, MASKED)
            m_prev = m_ref[mp]
            m_new = jnp.maximum(m_prev, jnp.max(s, axis=-1, keepdims=True))
            alpha = jnp.exp(m_prev - m_new)
            p = jnp.exp(s - m_new)
            l_ref[mp] = alpha * l_ref[mp] + jnp.sum(p, axis=-1, keepdims=True)
            acc_ref[mp] = alpha * acc_ref[mp] + _dot(p.astype(BF16), v)
            m_ref[mp] = m_new

    def body(ki, carry):
        off = pl.multiple_of(ki * tq, tq)
        step(k_ref[pl.ds(off, tq), :], v_ref[pl.ds(off, tq), :], None)
        return carry

    lax.fori_loop(0, qi, body, 0)

    off = pl.multiple_of(qi * tq, tq)
    row = lax.broadcasted_iota(jnp.int32, (tq, tq), 0)
    col = lax.broadcasted_iota(jnp.int32, (tq, tq), 1)
    step(k_ref[pl.ds(off, tq), :], v_ref[pl.ds(off, tq), :], (col // CHUNK) <= (row // CHUNK))

    o = acc_ref[0] / l_ref[0] - lam_ref[...] * (acc_ref[1] / l_ref[1])
    o_ref[...] = _rms(o, gain_ref[...]).astype(o_ref.dtype)


def _diff_attn(qa, qb, kr, p0, v_blk, lam_vec, gain, batch, seq, tq):
    t = qa.shape[0]
    nq = seq // tq
    kern = functools.partial(_diff_attn_kernel, tq=tq)
    qspec = pl.BlockSpec((tq, LANES), lambda b, h, q: (b * nq + q, h))
    return pl.pallas_call(
        kern,
        grid=(batch, DIFF_HEADS, nq),
        in_specs=[qspec, qspec,
                  pl.BlockSpec((seq, LANES), lambda b, h, q: (b, h)),
                  pl.BlockSpec((seq, LANES), lambda b, h, q: (b, v_blk + h)),
                  pl.BlockSpec((1, LANES), lambda b, h, q: (0, 0)),
                  pl.BlockSpec((1, LANES), lambda b, h, q: (0, 0))],
        out_specs=qspec,
        out_shape=jax.ShapeDtypeStruct((t, DIFF_HEADS * LANES), BF16),
        scratch_shapes=[pltpu.VMEM((2, tq, 1), F32),
                        pltpu.VMEM((2, tq, 1), F32),
                        pltpu.VMEM((2, tq, LANES), F32)],
        compiler_params=_params("parallel", "parallel", "arbitrary"),
        name="diff_attn",
    )(qa, qb, kr, p0, lam_vec, gain)


def _band_attn_kernel(q_ref, kc_ref, kp_ref, vc_ref, vp_ref, bias_ref, o_ref,
                      kw_ref, vw_ref, *, tq, tiles_per_batch):
    i = pl.program_id(0)
    pad = CA_LEFT_CHUNKS * CHUNK
    kw_ref[0:tq, :] = kp_ref[...]
    kw_ref[tq:, :] = kc_ref[...]
    vw_ref[0:tq, :] = vp_ref[...]
    vw_ref[tq:, :] = vc_ref[...]
    has_prev = (i % tiles_per_batch) != 0
    lane = lax.broadcasted_iota(jnp.int32, (BAND_SUB, LANES), 1)
    lo = lane < CA_HD
    col = lax.broadcasted_iota(jnp.int32, (BAND_SUB, BAND_WIN), 1)

    def sub_tile(s, carry):
        r0 = pl.multiple_of(s * BAND_SUB, BAND_SUB)
        w0 = pl.multiple_of(r0 + (tq - pad), BAND_SUB)
        valid = jnp.logical_or(has_prev, col + w0 >= tq)
        for pr in range(CA_HEADS // 2):
            cs = slice(pr * LANES, (pr + 1) * LANES)
            qp = q_ref[pl.ds(r0, BAND_SUB), cs]
            kw = kw_ref[pl.ds(w0, BAND_WIN), cs]
            vw = vw_ref[pl.ds(w0, BAND_WIN), cs]
            outs = []
            for hh in range(2):
                qh = jnp.where(lo, qp, 0) if hh == 0 else jnp.where(lo, 0, qp)
                sc = _dot_nt(qh.astype(BF16), kw) + bias_ref[2 * pr + hh]
                sc = jnp.where(valid, sc, MASKED)
                mx = jnp.max(sc, axis=-1, keepdims=True)
                p = jnp.exp(sc - mx)
                den = jnp.sum(p, axis=-1, keepdims=True)
                outs.append(_dot(p.astype(BF16), vw) / den)
            o_ref[pl.ds(r0, BAND_SUB), cs] = jnp.where(lo, outs[0], outs[1]).astype(o_ref.dtype)
        return carry

    lax.fori_loop(0, tq // BAND_SUB, sub_tile, 0)


def _band_attn(qkv, bias, seq, tq):
    t = qkv.shape[0]
    d = CA_HEADS * CA_HD
    tpb = seq // tq
    kern = functools.partial(_band_attn_kernel, tq=tq, tiles_per_batch=tpb)

    def prev(i):
        return jnp.where(i % tpb == 0, i, i - 1)

    return pl.pallas_call(
        kern,
        grid=(t // tq,),
        in_specs=[pl.BlockSpec((tq, d), lambda i: (i, 0)),
                  pl.BlockSpec((tq, d), lambda i: (i, 1)),
                  pl.BlockSpec((tq, d), lambda i: (prev(i), 1)),
                  pl.BlockSpec((tq, d), lambda i: (i, 2)),
                  pl.BlockSpec((tq, d), lambda i: (prev(i), 2)),
                  pl.BlockSpec(bias.shape, lambda i: (0, 0, 0))],
        out_specs=pl.BlockSpec((tq, d), lambda i: (i, 0)),
        out_shape=jax.ShapeDtypeStruct((t, d), BF16),
        scratch_shapes=[pltpu.VMEM((2 * tq, d), BF16),
                        pltpu.VMEM((2 * tq, d), BF16)],
        compiler_params=_params("parallel"),
        name="band_attn",
    )(qkv, qkv, qkv, qkv, qkv, bias)


def _deltanet_kernel(q_ref, k_ref, v_ref, qp_ref, kp_ref, vp_ref, ab_ref, gate_ref,
                     cw_ref, ea_ref, dtb_ref, ng_ref, o_ref, state_ref, xx_ref, *, tm):
    i = pl.program_id(1)
    pair = 2 * CHUNK

    @pl.when(i == 0)
    def _():
        state_ref[...] = jnp.zeros(state_ref.shape, F32)

    keep = (i > 0).astype(F32)
    hist = 16

    def conv_silu(cur_ref, prev_ref, c0):
        xx_ref[0:hist, :] = prev_ref[...].astype(F32) * keep
        cur = cur_ref[...].astype(F32)
        xx_ref[hist:, :] = cur
        w = cw_ref[:, c0:c0 + DN_QK]
        y = w[DN_CONV - 1:DN_CONV] * cur
        for s in range(1, DN_CONV):
            y = y + w[DN_CONV - 1 - s:DN_CONV - s] * xx_ref[pl.ds(hist - s, tm), :]
        return _silu(y)

    qc = conv_silu(q_ref, qp_ref, 0)
    kc = conv_silu(k_ref, kp_ref, DN_QK)
    vc = conv_silu(v_ref, vp_ref, 2 * DN_QK)

    ab = ab_ref[...]
    g_all = -ea_ref[...] * jax.nn.softplus(ab + dtb_ref[...])
    beta_all = jax.nn.sigmoid(ab)

    row = lax.broadcasted_iota(jnp.int32, (pair, pair), 0)
    col = lax.broadcasted_iota(jnp.int32, (pair, pair), 1)
    same = (row // CHUNK) == (col // CHUNK)
    tri = jnp.logical_and(same, col <= row)
    strict = jnp.logical_and(same, col < row)
    lmat = jnp.where(tri, 1.0, 0.0).astype(BF16)
    su = jnp.where(jnp.logical_and(same, row > col), 1.0, 0.0)
    eye = jnp.where(row == col, 1.0, 0.0)
    top = row < CHUNK

    for h in range(DN_HEADS):
        hs = slice(h * DN_DK, (h + 1) * DN_DK)
        qh = qc[:, hs]
        kh = kc[:, hs]
        qn = qh * lax.rsqrt(jnp.sum(qh * qh, axis=-1, keepdims=True) + RMS_EPS) * (DN_DK ** -0.5)
        kn = kh * lax.rsqrt(jnp.sum(kh * kh, axis=-1, keepdims=True) + RMS_EPS)
        g_b = jnp.broadcast_to(g_all[:, h:h + 1], (tm, pair))
        beta_b = jnp.broadcast_to(beta_all[:, DN_HEADS + h:DN_HEADS + h + 1], (tm, pair))
        st = state_ref[h]
        for dc in range(tm // pair):
            rs = slice(dc * pair, (dc + 1) * pair)
            q2, k2, v2, g2, b2 = qn[rs], kn[rs], vc[rs, hs], g_b[rs], beta_b[rs]
            gcum = _dot_exact_lhs(lmat, g2)
            gdiff = _dot_exact_lhs(lmat, g2 * su)
            decay = jnp.where(tri, jnp.exp(jnp.where(tri, gdiff, 0.0)), 0.0)
            a = jnp.where(strict, b2 * _dot_nt(k2.astype(BF16), k2.astype(BF16)) * decay, 0.0)
            tinv = eye - a
            pw = a
            for _ in range(5):
                pw = _dot_hp(pw, pw)
                tinv = tinv + _dot_hp(tinv, pw)
            eg = jnp.exp(gcum)
            u = _dot_hp(tinv, v2 * b2)
            w = _dot_hp(tinv, k2 * b2 * eg)
            qk = jnp.where(tri, _dot_nt(q2.astype(BF16), k2.astype(BF16)) * decay, 0.0)
            qd = q2 * eg
            gl = (gcum[CHUNK - 1:CHUNK], gcum[pair - 1:pair])
            kd = k2 * jnp.exp(jnp.where(top, gl[0], gl[1]) - gcum)
            vns, ois = [], []
            for c in range(2):
                cr = slice(c * CHUNK, (c + 1) * CHUNK)
                st_b = st.astype(BF16)
                vn = u[cr] - _dot(w[cr].astype(BF16), st_b)
                ois.append(_dot(qd[cr].astype(BF16), st_b))
                st = st * jnp.exp(gl[c]) + _dot_tn(kd[cr].astype(BF16), vn.astype(BF16))
                vns.append(vn)
            vn2 = jnp.concatenate(vns, axis=0)
            o = jnp.concatenate(ois, axis=0) + _dot(qk.astype(BF16), vn2.astype(BF16))
            o = _rms(o, ng_ref[...]) * _silu(gate_ref[rs, hs].astype(F32))
            o_ref[rs, hs] = o.astype(o_ref.dtype)
        state_ref[h] = st


def _deltanet(p0, ab, cwp, ea, dtb, ng, batch, seq, tm):
    t = p0.shape[0]
    w = DN_QK
    nt = seq // tm
    hist = 16
    kern = functools.partial(_deltanet_kernel, tm=tm)

    def cur(cb):
        return pl.BlockSpec((tm, w), lambda b, i: (b * nt + i, cb))

    def prev(cb):
        return pl.BlockSpec((hist, w), lambda b, i: (jnp.maximum((b * nt + i) * (tm // hist) - 1, 0), cb))

    vec = pl.BlockSpec((1, LANES), lambda b, i: (0, 0))
    return pl.pallas_call(
        kern,
        grid=(batch, nt),
        in_specs=[cur(0), cur(1), cur(2), prev(0), prev(1), prev(2),
                  pl.BlockSpec((tm, LANES), lambda b, i: (b * nt + i, 0)),
                  cur(3),
                  pl.BlockSpec(cwp.shape, lambda b, i: (0, 0)),
                  vec, vec, vec],
        out_specs=pl.BlockSpec((tm, w), lambda b, i: (b * nt + i, 0)),
        out_shape=jax.ShapeDtypeStruct((t, w), BF16),
        scratch_shapes=[pltpu.VMEM((DN_HEADS, DN_DK, DN_DK), F32),
                        pltpu.VMEM((tm + hist, w), F32)],
        compiler_params=_params("arbitrary", "arbitrary"),
        name="deltanet",
    )(p0, p0, p0, p0, p0, p0, ab, p0, cwp, ea, dtb, ng)


def _tile(n, want):
    return min(n, want)


def _pad_rows(a, rows):
    return jnp.concatenate([a, jnp.zeros((rows - a.shape[0], a.shape[1]), a.dtype)], axis=0)


def kernel(x, even_norm_mix, even_w_in, even_dn_conv, even_dn_A_log, even_dn_dt_bias, even_dn_norm,
           even_diff_lambda, even_diff_subln, even_w_out, odd_norm_mix, odd_w_qkv, odd_rel_bias,
           odd_w_out, ffn_norm, ffn_w_in, ffn_conv_w, ffn_conv_b, ffn_w_out, final_norm):
    batch, seq, d = x.shape
    t = batch * seq
    xr = x.reshape(t, d)
    tm = _tile(seq, 512)

    def ffn(xr, i, final):
        cwb = _pad_rows(jnp.concatenate([ffn_conv_w[i], ffn_conv_b[i][None, :]], axis=0), 8)
        return _ffn(xr, ffn_norm[i][None, :], ffn_w_in[i].astype(BF16), cwb,
                    ffn_w_out[i].astype(BF16), final_norm[None, :], seq, tm, 256, final)

    w0 = even_w_in[0]
    n_qkv = 3 * DN_QK
    w_main = jnp.concatenate([w0[:, :n_qkv], w0[:, n_qkv + 2 * DN_HEADS:]], axis=1).astype(BF16)
    w_ab = jnp.pad(w0[:, n_qkv:n_qkv + 2 * DN_HEADS], ((0, 0), (0, LANES - 2 * DN_HEADS)))
    w_ab_hi = w_ab.astype(BF16)
    w_ab_lo = (w_ab - w_ab_hi.astype(F32)).astype(BF16)
    g0 = even_norm_mix[0][None, :]
    p0 = _norm_matmul(xr, g0, w_main, BF16, tm, 512)
    ab = _norm_matmul_hp(xr, g0, w_ab_hi, w_ab_lo, tm)

    def lane_vec(v):
        return jnp.pad(v.astype(F32), (0, LANES - v.shape[0]))[None, :]

    o_a = _deltanet(p0, ab, _pad_rows(even_dn_conv[0], 8), lane_vec(jnp.exp(even_dn_A_log[0])),
                    lane_vec(even_dn_dt_bias[0]), even_dn_norm[0][None, :], batch, seq, _tile(seq, 256))

    half = DIFF_HD // 2
    inv = 1.0 / (ROPE_THETA ** (jnp.arange(0, DIFF_HD, 2, dtype=F32) / DIFF_HD))
    ang = jnp.arange(seq, dtype=F32)[:, None] * inv[None, :]
    cos_t = jnp.tile(jnp.cos(ang), (1, LANES // half))
    sign = jnp.tile(jnp.concatenate([-jnp.ones((half,), F32), jnp.ones((half,), F32)]), LANES // DIFF_HD)
    sin_t = jnp.tile(jnp.sin(ang), (1, LANES // half)) * sign[None, :]
    qa, qb, kr = _rope(p0, cos_t, sin_t, seq, tm, 16, 20)

    lam_init = 0.8 - 0.6 * math.exp(-0.3 * 0)
    lp = even_diff_lambda[0].astype(F32)
    lam = jnp.exp(jnp.sum(lp[0] * lp[1])) - jnp.exp(jnp.sum(lp[2] * lp[3])) + lam_init
    lam_vec = jnp.full((1, LANES), lam, F32)
    gain_b = (even_diff_subln[0] * (1.0 - lam_init))[None, :]
    o_b = _diff_attn(qa, qb, kr, p0, 24, lam_vec, gain_b, batch, seq, tm)

    wo = even_w_out[0].astype(BF16)
    xr = _proj_res([o_a, o_b], [wo[:DN_QK], wo[DN_QK:]], xr, tm)
    xr = ffn(xr, 0, False)

    wq = odd_w_qkv[0]
    wq = jnp.concatenate([wq[:, :D_MODEL] * (CA_HD ** -0.5), wq[:, D_MODEL:]], axis=1).astype(BF16)
    qkv = _norm_matmul(xr, odd_norm_mix[0][None, :], wq, BF16, tm, 512)
    pad = CA_LEFT_CHUNKS * CHUNK
    qi = jnp.arange(BAND_SUB)[:, None]
    kj = jnp.arange(BAND_WIN)[None, :]
    rel = kj - pad - qi
    in_band = jnp.logical_and(kj // CHUNK >= qi // CHUNK, kj // CHUNK <= qi // CHUNK + CA_LEFT_CHUNKS)
    bias = odd_rel_bias[0].astype(F32)[:, jnp.clip(rel, -REL_CLIP, REL_CLIP) + REL_CLIP]
    bias = jnp.where(in_band[None], bias, MASKED)
    o_c = _band_attn(qkv, bias, seq, tm)
    xr = _proj_res([o_c], [odd_w_out[0].astype(BF16)], xr, tm)
    xr = ffn(xr, 1, True)
    return xr.reshape(batch, seq, d)
```

```python
import functools
import math

import jax
import jax.numpy as jnp
from jax import lax
from jax.experimental import pallas as pl
from jax.experimental.pallas import tpu as pltpu

F32 = jnp.float32
BF16 = jnp.bfloat16

D_MODEL = 1024
CHUNK = 64
RMS_EPS = 1e-6
ROPE_THETA = 10000.0
DN_HEADS = 4
DN_DK = 128
DN_QK = 512
DN_CONV = 4
DIFF_HEADS = 4
DIFF_HD = 64
CA_HEADS = 16
CA_HD = 64
CA_LEFT_CHUNKS = 8
REL_CLIP = 256
D_FF = 2816
MASKED = -1e30
VMEM_LIMIT = 56 * 1024 * 1024
LANES = 128
QCOLS = 256
BAND_SUB = 128
BAND_WIN = BAND_SUB + CA_LEFT_CHUNKS * CHUNK


def _params(*sem):
    return pltpu.CompilerParams(dimension_semantics=sem, vmem_limit_bytes=VMEM_LIMIT)


def _rms(x, g):
    ms = jnp.mean(x * x, axis=-1, keepdims=True)
    return x * lax.rsqrt(ms + RMS_EPS) * g


def _silu(x):
    return x * jax.nn.sigmoid(x)


def _dot(a, b):
    return jnp.dot(a, b, preferred_element_type=F32)


def _dot_nt(a, b):
    return lax.dot_general(a, b, (((1,), (1,)), ((), ())), preferred_element_type=F32)


def _dot_tn(a, b):
    return lax.dot_general(a, b, (((0,), (0,)), ((), ())), preferred_element_type=F32)


def _split(x):
    hi = x.astype(BF16)
    lo = (x - hi.astype(F32)).astype(BF16)
    return hi, lo


def _dot_exact_lhs(l_bf16, x):
    x1 = x.astype(BF16)
    r1 = x - x1.astype(F32)
    x2 = r1.astype(BF16)
    x3 = (r1 - x2.astype(F32)).astype(BF16)
    return _dot(l_bf16, x1) + (_dot(l_bf16, x2) + _dot(l_bf16, x3))


def _resident(shape):
    return pl.BlockSpec(shape, lambda *_: (0,) * len(shape), pipeline_mode=pl.Buffered(1))


def _norm_matmul_kernel(x_ref, g_ref, w_ref, o_ref, *, tn):
    xn = _rms(x_ref[...], g_ref[...]).astype(BF16)
    for n0 in range(0, w_ref.shape[1], tn):
        o_ref[:, n0:n0 + tn] = _dot(xn, w_ref[:, n0:n0 + tn]).astype(o_ref.dtype)


def _norm_matmul(x, g, w, out_dtype, tm, tn):
    t, d = x.shape
    n = w.shape[1]
    return pl.pallas_call(
        functools.partial(_norm_matmul_kernel, tn=tn),
        grid=(t // tm,),
        in_specs=[pl.BlockSpec((tm, d), lambda i: (i, 0)),
                  _resident((1, d)),
                  _resident(w.shape)],
        out_specs=pl.BlockSpec((tm, n), lambda i: (i, 0)),
        out_shape=jax.ShapeDtypeStruct((t, n), out_dtype),
        compiler_params=_params("parallel"),
        name="norm_matmul",
    )(x, g, w)


def _norm_matmul_hp_kernel(x_ref, g_ref, whi_ref, wlo_ref, o_ref):
    xn = _rms(x_ref[...], g_ref[...])
    hi, lo = _split(xn)
    o_ref[...] = _dot(hi, whi_ref[...]) + (_dot(hi, wlo_ref[...]) + _dot(lo, whi_ref[...]))


def _norm_matmul_hp(x, g, w_hi, w_lo, tm):
    t, d = x.shape
    n = w_hi.shape[1]
    return pl.pallas_call(
        _norm_matmul_hp_kernel,
        grid=(t // tm,),
        in_specs=[pl.BlockSpec((tm, d), lambda i: (i, 0)),
                  pl.BlockSpec((1, d), lambda i: (0, 0)),
                  pl.BlockSpec((d, n), lambda i: (0, 0)),
                  pl.BlockSpec((d, n), lambda i: (0, 0))],
        out_specs=pl.BlockSpec((tm, n), lambda i: (i, 0)),
        out_shape=jax.ShapeDtypeStruct((t, n), F32),
        compiler_params=_params("parallel"),
        name="norm_matmul_hp",
    )(x, g, w_hi, w_lo)


def _proj_res_kernel(*refs, trans):
    n_in = len(trans)
    a_refs = refs[:n_in]
    w_refs = refs[n_in:2 * n_in]
    res_ref, o_ref = refs[2 * n_in:]
    acc = res_ref[...]
    for a_ref, w_ref, tr in zip(a_refs, w_refs, trans):
        acc = acc + (_dot_tn if tr else _dot)(a_ref[...], w_ref[...])
    o_ref[...] = acc


def _proj_res(acts, ws, res, seq, tm):
    t, d = res.shape
    tpb = seq // tm
    trans = tuple(a.ndim == 3 for a in acts)
    in_specs = ([pl.BlockSpec((None, a.shape[1], tm), lambda i: (i // tpb, 0, i % tpb)) if tr
                 else pl.BlockSpec((tm, a.shape[1]), lambda i: (i, 0)) for a, tr in zip(acts, trans)]
                + [pl.BlockSpec(w.shape, lambda i: (0, 0)) for w in ws]
                + [pl.BlockSpec((tm, d), lambda i: (i, 0))])
    return pl.pallas_call(
        functools.partial(_proj_res_kernel, trans=trans),
        grid=(t // tm,),
        in_specs=in_specs,
        out_specs=pl.BlockSpec((tm, d), lambda i: (i, 0)),
        out_shape=jax.ShapeDtypeStruct((t, d), F32),
        compiler_params=_params("parallel"),
        name="proj_res",
    )(*acts, *ws, res)


def _ffn_kernel(x_ref, g_ref, wi_ref, cw_ref, wo_ref, fg_ref, o_ref, gs_ref, act_ref,
                *, tm, tf, tiles_per_batch, final_norm):
    i = pl.program_id(0)

    @pl.when(i % tiles_per_batch == 0)
    def _():
        gs_ref[0:8, :] = jnp.zeros((8, D_FF), F32)

    x = x_ref[...]
    xn = _rms(x, g_ref[...]).astype(BF16)
    for c0 in range(0, D_FF, tf):
        cs = slice(c0, c0 + tf)
        gate = _dot(xn, wi_ref[:, cs])
        up = _dot(xn, wi_ref[:, D_FF + c0:D_FF + c0 + tf])
        gs_ref[8:, cs] = gate
        cw = cw_ref[:, cs]
        y = (cw[2:3] * gate + cw[1:2] * gs_ref[pl.ds(7, tm), cs]
             + cw[0:1] * gs_ref[pl.ds(6, tm), cs] + cw[3:4])
        act_ref[:, cs] = (_silu(y) * up).astype(BF16)
    gs_ref[0:8, :] = gs_ref[tm:tm + 8, :]
    r = x + _dot(act_ref[...], wo_ref[...])
    o_ref[...] = _rms(r, fg_ref[...]) if final_norm else r


def _ffn(x, g, w_in, cwb, w_out, final_g, seq, tm, tf, final_norm):
    t, d = x.shape
    kern = functools.partial(_ffn_kernel, tm=tm, tf=tf, tiles_per_batch=seq // tm, final_norm=final_norm)
    return pl.pallas_call(
        kern,
        grid=(t // tm,),
        in_specs=[pl.BlockSpec((tm, d), lambda i: (i, 0)),
                  _resident((1, d)),
                  _resident(w_in.shape),
                  _resident(cwb.shape),
                  _resident(w_out.shape),
                  _resident((1, d))],
        out_specs=pl.BlockSpec((tm, d), lambda i: (i, 0)),
        out_shape=jax.ShapeDtypeStruct((t, d), F32),
        scratch_shapes=[pltpu.VMEM((tm + 8, D_FF), F32),
                        pltpu.VMEM((tm, D_FF), BF16)],
        compiler_params=_params("arbitrary"),
        name="ffn",
    )(x, g, w_in, cwb, w_out, final_g)


def _rope_kernel(q_ref, k_ref, v_ref, cos_ref, sin_ref, qa_ref, qb_ref, kr_ref, vt_ref):
    cos = cos_ref[...]
    sin = sin_ref[...]
    lane = lax.broadcasted_iota(jnp.int32, cos.shape, 1)
    first = (lane % DIFF_HD) < (DIFF_HD // 2)

    def rope(x):
        rot = jnp.where(first, pltpu.roll(x, LANES - DIFF_HD // 2, 1), pltpu.roll(x, DIFF_HD // 2, 1))
        return x * cos + rot * sin

    q = rope(q_ref[...].astype(F32)) * (DIFF_HD ** -0.5 * math.log2(math.e))
    k = rope(k_ref[...].astype(F32))
    lo = lane < DIFF_HD
    qa_ref[...] = jnp.where(lo, q, 0.0).astype(BF16)
    qb_ref[...] = jnp.where(lo, 0.0, q).astype(BF16)
    kr_ref[...] = k.astype(BF16)
    vt_ref[...] = v_ref[...].astype(F32).T.astype(BF16)


def _rope(p0, cos_t, sin_t, batch, seq, tm, q_blk, k_blk, v_blk):
    t = p0.shape[0]
    w = DIFF_HEADS * LANES
    out = jax.ShapeDtypeStruct((t, w), BF16)
    spt = seq // tm
    return pl.pallas_call(
        _rope_kernel,
        grid=(t // tm, DIFF_HEADS),
        in_specs=[pl.BlockSpec((tm, LANES), lambda i, h: (i, q_blk + h)),
                  pl.BlockSpec((tm, LANES), lambda i, h: (i, k_blk + h)),
                  pl.BlockSpec((tm, LANES), lambda i, h: (i, v_blk + h)),
                  pl.BlockSpec((tm, LANES), lambda i, h: (i % spt, 0)),
                  pl.BlockSpec((tm, LANES), lambda i, h: (i % spt, 0))],
        out_specs=[pl.BlockSpec((tm, LANES), lambda i, h: (i, h))] * 3
        + [pl.BlockSpec((None, None, None, LANES, tm), lambda i, h: (i // spt, h, i % spt, 0, 0))],
        out_shape=[out, out, out, jax.ShapeDtypeStruct((batch, DIFF_HEADS, spt, LANES, tm), BF16)],
        compiler_params=_params("parallel", "parallel"),
        name="rope",
    )(p0, p0, p0, cos_t, sin_t)


def _diff_attn_kernel(qa_ref, qb_ref, k_ref, vt_ref, lam_ref, gain_ref, o_ref, acc_ref, *, tq, tk):
    qi = pl.program_id(2)
    acc_ref[...] = jnp.zeros(acc_ref.shape, F32)
    qs = (qa_ref[...], qb_ref[...])

    n_sub = tq // tk

    def step(ki, carry, mask):
        off = pl.multiple_of(ki * tk, tk)
        kt = k_ref[pl.ds(off, tk), :]
        vt = vt_ref[ki]
        ss = [_dot_nt(kt, qs[mp]) for mp in range(2)]
        new = []
        for mp in range(2):
            ms, ls = [], []
            for c0 in range(0, tq, QCOLS):
                cs = slice(c0, c0 + QCOLS)
                m_prev, l_prev = carry[2 * mp][:, cs], carry[2 * mp + 1][:, cs]
                s = ss[mp][:, cs]
                if mask is not None:
                    s = jnp.where(mask[:, cs], s, MASKED)
                m_new = jnp.maximum(m_prev, jnp.max(s, axis=0, keepdims=True))
                alpha = jnp.exp2(m_prev - m_new)
                p = jnp.exp2(s - m_new)
                ms.append(m_new)
                ls.append(alpha * l_prev + jnp.sum(p, axis=0, keepdims=True))
                acc_ref[mp, :, cs] = alpha * acc_ref[mp, :, cs] + _dot(vt, p.astype(BF16))
            new += [jnp.concatenate(ms, axis=1), jnp.concatenate(ls, axis=1)]
        return tuple(new)

    row0 = jnp.full((1, tq), MASKED, F32)
    zero = jnp.zeros((1, tq), F32)

    def body(kj, c):
        for sub in range(n_sub):
            c = step(kj * n_sub + sub, c, None)
        return c

    carry = lax.fori_loop(0, qi, body, (row0, zero, row0, zero))

    key = lax.broadcasted_iota(jnp.int32, (tk, tq), 0)
    qry = lax.broadcasted_iota(jnp.int32, (tk, tq), 1)
    for sub in range(n_sub):
        carry = step(qi * n_sub + sub, carry, (key // CHUNK + sub * (tk // CHUNK)) <= (qry // CHUNK))
    _, l1, _, l2 = carry

    o = acc_ref[0] / l1 - lam_ref[...] * (acc_ref[1] / l2)
    ms = jnp.mean(o * o, axis=0, keepdims=True)
    o_ref[...] = (o * lax.rsqrt(ms + RMS_EPS) * gain_ref[...]).astype(o_ref.dtype)


def _diff_attn(qa, qb, kr, vt, lam_row, gain_t, batch, seq, tq):
    nq = seq // tq
    nk, _, tk = vt.shape[2:]
    kern = functools.partial(_diff_attn_kernel, tq=tq, tk=tk)
    qspec = pl.BlockSpec((tq, LANES), lambda b, h, q: (b * nq + q, h))
    return pl.pallas_call(
        kern,
        grid=(batch, DIFF_HEADS, nq),
        in_specs=[qspec, qspec,
                  pl.BlockSpec((seq, LANES), lambda b, h, q: (b, h)),
                  pl.BlockSpec((None, None, nk, LANES, tk), lambda b, h, q: (b, h, 0, 0, 0)),
                  pl.BlockSpec((1, tq), lambda b, h, q: (0, 0)),
                  pl.BlockSpec((LANES, tq), lambda b, h, q: (0, 0))],
        out_specs=pl.BlockSpec((None, LANES, tq), lambda b, h, q: (b, h, q)),
        out_shape=jax.ShapeDtypeStruct((batch, DIFF_HEADS * LANES, seq), BF16),
        scratch_shapes=[pltpu.VMEM((2, LANES, tq), F32)],
        compiler_params=_params("parallel", "parallel", "arbitrary"),
        name="diff_attn",
    )(qa, qb, kr, vt, lam_row, gain_t)


def _band_attn_kernel(q_ref, kc_ref, kp_ref, vc_ref, vp_ref, bias_ref, o_ref,
                      kw_ref, vw_ref, *, tq, tiles_per_batch):
    i = pl.program_id(0)
    pad = CA_LEFT_CHUNKS * CHUNK
    kw_ref[0:tq, :] = kp_ref[...]
    kw_ref[tq:, :] = kc_ref[...]
    vw_ref[0:tq, :] = vp_ref[...]
    vw_ref[tq:, :] = vc_ref[...]
    has_prev = (i % tiles_per_batch) != 0
    lane = lax.broadcasted_iota(jnp.int32, (BAND_SUB, LANES), 1)
    lo = lane < CA_HD
    col = lax.broadcasted_iota(jnp.int32, (BAND_SUB, BAND_WIN), 1)

    def sub_tile(s, carry):
        r0 = pl.multiple_of(s * BAND_SUB, BAND_SUB)
        w0 = pl.multiple_of(r0 + (tq - pad), BAND_SUB)
        valid = jnp.logical_or(has_prev, col + w0 >= tq)
        css = [slice(pr * LANES, (pr + 1) * LANES) for pr in range(CA_HEADS // 2)]
        scs = []
        for pr, cs in enumerate(css):
            qp = q_ref[pl.ds(r0, BAND_SUB), cs]
            kw = kw_ref[pl.ds(w0, BAND_WIN), cs]
            for hh in range(2):
                qh = jnp.where(lo, qp, 0) if hh == 0 else jnp.where(lo, 0, qp)
                scs.append(_dot_nt(qh.astype(BF16), kw))
        ps, dens = [], []
        for h, sc in enumerate(scs):
            sc = jnp.where(valid, sc + bias_ref[h], MASKED)
            p = jnp.exp(sc - jnp.max(sc, axis=-1, keepdims=True))
            dens.append(jnp.sum(p, axis=-1, keepdims=True))
            ps.append(p.astype(BF16))
        for pr, cs in enumerate(css):
            vw = vw_ref[pl.ds(w0, BAND_WIN), cs]
            outs = [_dot(ps[2 * pr + hh], vw) / dens[2 * pr + hh] for hh in range(2)]
            o_ref[pl.ds(r0, BAND_SUB), cs] = jnp.where(lo, outs[0], outs[1]).astype(o_ref.dtype)
        return carry

    lax.fori_loop(0, tq // BAND_SUB, sub_tile, 0)


def _band_attn(qkv, bias, seq, tq):
    t = qkv.shape[0]
    d = CA_HEADS * CA_HD
    tpb = seq // tq
    kern = functools.partial(_band_attn_kernel, tq=tq, tiles_per_batch=tpb)

    def prev(i):
        return jnp.where(i % tpb == 0, i, i - 1)

    return pl.pallas_call(
        kern,
        grid=(t // tq,),
        in_specs=[pl.BlockSpec((tq, d), lambda i: (i, 0)),
                  pl.BlockSpec((tq, d), lambda i: (i, 1)),
                  pl.BlockSpec((tq, d), lambda i: (prev(i), 1)),
                  pl.BlockSpec((tq, d), lambda i: (i, 2)),
                  pl.BlockSpec((tq, d), lambda i: (prev(i), 2)),
                  pl.BlockSpec(bias.shape, lambda i: (0, 0, 0))],
        out_specs=pl.BlockSpec((tq, d), lambda i: (i, 0)),
        out_shape=jax.ShapeDtypeStruct((t, d), BF16),
        scratch_shapes=[pltpu.VMEM((2 * tq, d), BF16),
                        pltpu.VMEM((2 * tq, d), BF16)],
        compiler_params=_params("parallel"),
        name="band_attn",
    )(qkv, qkv, qkv, qkv, qkv, bias)


def _deltanet_kernel(q_ref, k_ref, v_ref, qp_ref, kp_ref, vp_ref, ab_ref, gate_ref,
                     cw_ref, ea_ref, dtb_ref, ng_ref, o_ref, state_ref, xx_ref, *, tm):
    i = pl.program_id(1)
    pair = 2 * CHUNK

    @pl.when(i == 0)
    def _():
        state_ref[...] = jnp.zeros(state_ref.shape, F32)

    keep = (i > 0).astype(F32)
    hist = 16

    def conv_silu(cur_ref, prev_ref, c0):
        xx_ref[0:hist, :] = prev_ref[...].astype(F32) * keep
        cur = cur_ref[...].astype(F32)
        xx_ref[hist:, :] = cur
        w = cw_ref[:, c0:c0 + DN_QK]
        y = w[DN_CONV - 1:DN_CONV] * cur
        for s in range(1, DN_CONV):
            y = y + w[DN_CONV - 1 - s:DN_CONV - s] * xx_ref[pl.ds(hist - s, tm), :]
        return _silu(y)

    qc = conv_silu(q_ref, qp_ref, 0)
    kc = conv_silu(k_ref, kp_ref, DN_QK)
    vc = conv_silu(v_ref, vp_ref, 2 * DN_QK)

    ab = ab_ref[...]
    g_all = -ea_ref[...] * jax.nn.softplus(ab + dtb_ref[...])
    beta_all = jax.nn.sigmoid(ab)

    row = lax.broadcasted_iota(jnp.int32, (pair, pair), 0)
    col = lax.broadcasted_iota(jnp.int32, (pair, pair), 1)
    same = (row // CHUNK) == (col // CHUNK)
    tri = jnp.logical_and(same, col <= row)
    strict = jnp.logical_and(same, col < row)
    lmat = jnp.where(tri, 1.0, 0.0).astype(BF16)
    su = jnp.where(jnp.logical_and(same, row > col), 1.0, 0.0)
    eye = jnp.where(row == col, 1.0, 0.0)
    top = row < CHUNK

    npair = tm // pair
    units = [(h, dc) for h in range(DN_HEADS) for dc in range(npair)]
    nu = len(units)
    hsl = [slice(h * DN_DK, (h + 1) * DN_DK) for h in range(DN_HEADS)]
    rsl = [slice(dc * pair, (dc + 1) * pair) for dc in range(npair)]

    def bf(x):
        return x.astype(BF16)

    qn, kn, g_b, beta_b = [], [], [], []
    for h in range(DN_HEADS):
        qh = qc[:, hsl[h]]
        kh = kc[:, hsl[h]]
        qn.append(qh * lax.rsqrt(jnp.sum(qh * qh, axis=-1, keepdims=True) + RMS_EPS) * (DN_DK ** -0.5))
        kn.append(kh * lax.rsqrt(jnp.sum(kh * kh, axis=-1, keepdims=True) + RMS_EPS))
        g_b.append(jnp.broadcast_to(g_all[:, h:h + 1], (tm, pair)))
        beta_b.append(jnp.broadcast_to(beta_all[:, DN_HEADS + h:DN_HEADS + h + 1], (tm, pair)))
    q2 = [qn[h][rsl[dc]] for h, dc in units]
    k2 = [kn[h][rsl[dc]] for h, dc in units]
    v2 = [vc[rsl[dc], hsl[h]] for h, dc in units]
    b2 = [beta_b[h][rsl[dc]] for h, dc in units]
    k2b = [bf(x) for x in k2]

    g_cat = jnp.concatenate([g_b[h][rsl[dc]] for h, dc in units], axis=1)
    gcum_cat = _dot_exact_lhs(lmat, g_cat)
    gdiff_cat = _dot_exact_lhs(lmat, g_cat * jnp.concatenate([su] * nu, axis=1))
    usl = [slice(u * pair, (u + 1) * pair) for u in range(nu)]
    gcum = [gcum_cat[:, s] for s in usl]
    decay = [jnp.where(tri, jnp.exp(jnp.where(tri, gdiff_cat[:, s], 0.0)), 0.0) for s in usl]
    kk = [_dot_nt(x, x) for x in k2b]
    a = [jnp.where(strict, b2[u] * kk[u] * decay[u], 0.0) for u in range(nu)]
    tinv = [eye - x for x in a]
    pw = a
    for _ in range(5):
        pwb = [bf(x) for x in pw]
        pw = [_dot(x, x) for x in pwb]
        pwb = [bf(x) for x in pw]
        tinv = [tinv[u] + _dot(bf(tinv[u]), pwb[u]) for u in range(nu)]
    eg = [jnp.exp(x) for x in gcum]
    rhs = [jnp.concatenate([v2[u] * b2[u], k2[u] * b2[u] * eg[u]], axis=1) for u in range(nu)]
    uw = [_dot(bf(tinv[u]), bf(rhs[u])) for u in range(nu)]
    qk = [jnp.where(tri, _dot_nt(bf(q2[u]), k2b[u]) * decay[u], 0.0) for u in range(nu)]
    qd = [bf(q2[u] * eg[u]) for u in range(nu)]
    gl = [(gcum[u][CHUNK - 1:CHUNK], gcum[u][pair - 1:pair]) for u in range(nu)]
    kd = [bf(k2[u] * jnp.exp(jnp.where(top, gl[u][0], gl[u][1]) - gcum[u])) for u in range(nu)]
    egl = [(jnp.exp(gl[u][0]), jnp.exp(gl[u][1])) for u in range(nu)]

    st = [state_ref[h] for h in range(DN_HEADS)]
    vns = [[None, None] for _ in range(nu)]
    ois = [[None, None] for _ in range(nu)]
    for dc in range(npair):
        for c in range(2):
            cr = slice(c * CHUNK, (c + 1) * CHUNK)
            us = [h * npair + dc for h in range(DN_HEADS)]
            stb = [bf(x) for x in st]
            vn = [uw[u][cr, :DN_DK] - _dot(bf(uw[u][cr, DN_DK:]), stb[h]) for h, u in enumerate(us)]
            for h, u in enumerate(us):
                ois[u][c] = _dot(qd[u][cr], stb[h])
                vns[u][c] = vn[h]
            st = [st[h] * egl[u][c] + _dot_tn(kd[u][cr], bf(vn[h])) for h, u in enumerate(us)]
    for h in range(DN_HEADS):
        state_ref[h] = st[h]

    for u, (h, dc) in enumerate(units):
        vn2 = jnp.concatenate(vns[u], axis=0)
        o = jnp.concatenate(ois[u], axis=0) + _dot(bf(qk[u]), bf(vn2))
        o = _rms(o, ng_ref[...]) * _silu(gate_ref[rsl[dc], hsl[h]].astype(F32))
        o_ref[rsl[dc], hsl[h]] = o.astype(o_ref.dtype)


def _deltanet(p0, ab, cwp, ea, dtb, ng, batch, seq, tm):
    t = p0.shape[0]
    w = DN_QK
    nt = seq // tm
    hist = 16
    kern = functools.partial(_deltanet_kernel, tm=tm)

    def cur(cb):
        return pl.BlockSpec((tm, w), lambda b, i: (b * nt + i, cb))

    def prev(cb):
        return pl.BlockSpec((hist, w), lambda b, i: (jnp.maximum((b * nt + i) * (tm // hist) - 1, 0), cb))

    vec = pl.BlockSpec((1, LANES), lambda b, i: (0, 0))
    return pl.pallas_call(
        kern,
        grid=(batch, nt),
        in_specs=[cur(0), cur(1), cur(2), prev(0), prev(1), prev(2),
                  pl.BlockSpec((tm, LANES), lambda b, i: (b * nt + i, 0)),
                  cur(3),
                  pl.BlockSpec(cwp.shape, lambda b, i: (0, 0)),
                  vec, vec, vec],
        out_specs=pl.BlockSpec((tm, w), lambda b, i: (b * nt + i, 0)),
        out_shape=jax.ShapeDtypeStruct((t, w), BF16),
        scratch_shapes=[pltpu.VMEM((DN_HEADS, DN_DK, DN_DK), F32),
                        pltpu.VMEM((tm + hist, w), F32)],
        compiler_params=_params("arbitrary", "arbitrary"),
        name="deltanet",
    )(p0, p0, p0, p0, p0, p0, ab, p0, cwp, ea, dtb, ng)


def _tile(n, want):
    return min(n, want)


def _pad_rows(a, rows):
    return jnp.concatenate([a, jnp.zeros((rows - a.shape[0], a.shape[1]), a.dtype)], axis=0)


def kernel(x, even_norm_mix, even_w_in, even_dn_conv, even_dn_A_log, even_dn_dt_bias, even_dn_norm,
           even_diff_lambda, even_diff_subln, even_w_out, odd_norm_mix, odd_w_qkv, odd_rel_bias,
           odd_w_out, ffn_norm, ffn_w_in, ffn_conv_w, ffn_conv_b, ffn_w_out, final_norm):
    batch, seq, d = x.shape
    t = batch * seq
    xr = x.reshape(t, d)
    tm = _tile(seq, 512)

    def ffn(xr, i, final):
        cwb = _pad_rows(jnp.concatenate([ffn_conv_w[i], ffn_conv_b[i][None, :]], axis=0), 8)
        return _ffn(xr, ffn_norm[i][None, :], ffn_w_in[i].astype(BF16), cwb,
                    ffn_w_out[i].astype(BF16), final_norm[None, :], seq, tm, 256, final)

    w0 = even_w_in[0]
    n_qkv = 3 * DN_QK
    w_main = jnp.concatenate([w0[:, :n_qkv], w0[:, n_qkv + 2 * DN_HEADS:]], axis=1).astype(BF16)
    w_ab = jnp.pad(w0[:, n_qkv:n_qkv + 2 * DN_HEADS], ((0, 0), (0, LANES - 2 * DN_HEADS)))
    w_ab_hi = w_ab.astype(BF16)
    w_ab_lo = (w_ab - w_ab_hi.astype(F32)).astype(BF16)
    g0 = even_norm_mix[0][None, :]
    p0 = _norm_matmul(xr, g0, w_main, BF16, tm, 512)
    ab = _norm_matmul_hp(xr, g0, w_ab_hi, w_ab_lo, tm)

    def lane_vec(v):
        return jnp.pad(v.astype(F32), (0, LANES - v.shape[0]))[None, :]

    o_a = _deltanet(p0, ab, _pad_rows(even_dn_conv[0], 8), lane_vec(jnp.exp(even_dn_A_log[0])),
                    lane_vec(even_dn_dt_bias[0]), even_dn_norm[0][None, :], batch, seq, _tile(seq, 512))

    half = DIFF_HD // 2
    inv = 1.0 / (ROPE_THETA ** (jnp.arange(0, DIFF_HD, 2, dtype=F32) / DIFF_HD))
    ang = jnp.arange(seq, dtype=F32)[:, None] * inv[None, :]
    cos_t = jnp.tile(jnp.cos(ang), (1, LANES // half))
    sign = jnp.tile(jnp.concatenate([-jnp.ones((half,), F32), jnp.ones((half,), F32)]), LANES // DIFF_HD)
    sin_t = jnp.tile(jnp.sin(ang), (1, LANES // half)) * sign[None, :]
    qa, qb, kr, vt = _rope(p0, cos_t, sin_t, batch, seq, tm, 16, 20, 24)

    lam_init = 0.8 - 0.6 * math.exp(-0.3 * 0)
    lp = even_diff_lambda[0].astype(F32)
    lam = jnp.exp(jnp.sum(lp[0] * lp[1])) - jnp.exp(jnp.sum(lp[2] * lp[3])) + lam_init
    tq = _tile(seq, 1024)
    lam_row = jnp.full((1, tq), lam, F32)
    gain_t = jnp.broadcast_to((even_diff_subln[0] * (1.0 - lam_init))[:, None], (LANES, tq))
    o_b = _diff_attn(qa, qb, kr, vt, lam_row, gain_t, batch, seq, tq)

    wo = even_w_out[0].astype(BF16)
    xr = _proj_res([o_a, o_b], [wo[:DN_QK], wo[DN_QK:]], xr, seq, tm)
    xr = ffn(xr, 0, False)

    wq = odd_w_qkv[0]
    wq = jnp.concatenate([wq[:, :D_MODEL] * (CA_HD ** -0.5), wq[:, D_MODEL:]], axis=1).astype(BF16)
    qkv = _norm_matmul(xr, odd_norm_mix[0][None, :], wq, BF16, tm, 512)
    pad = CA_LEFT_CHUNKS * CHUNK
    qi = jnp.arange(BAND_SUB)[:, None]
    kj = jnp.arange(BAND_WIN)[None, :]
    in_band = jnp.logical_and(kj // CHUNK >= qi // CHUNK, kj // CHUNK <= qi // CHUNK + CA_LEFT_CHUNKS)
    m_len = BAND_WIN + BAND_SUB + 1
    rel = jnp.arange(m_len) - BAND_SUB - pad
    e = odd_rel_bias[0].astype(F32)[:, jnp.clip(rel, -REL_CLIP, REL_CLIP) + REL_CLIP]
    bias = jnp.tile(e, (1, BAND_SUB))[:, :BAND_SUB * (m_len - 1)].reshape(CA_HEADS, BAND_SUB, m_len - 1)
    bias = jnp.where(in_band[None], bias[:, :, BAND_SUB:BAND_SUB + BAND_WIN], MASKED)
    o_c = _band_attn(qkv, bias, seq, tm)
    xr = _proj_res([o_c], [odd_w_out[0].astype(BF16)], xr, seq, tm)
    xr = ffn(xr, 1, True)
    return xr.reshape(batch, seq, d)
```

```python
import functools
import math

import jax
import jax.numpy as jnp
from jax import lax
from jax.experimental import pallas as pl
from jax.experimental.pallas import tpu as pltpu

F32 = jnp.float32
BF16 = jnp.bfloat16

D_MODEL = 1024
CHUNK = 64
RMS_EPS = 1e-6
ROPE_THETA = 10000.0
DN_HEADS = 4
DN_DK = 128
DN_QK = 512
DN_CONV = 4
DIFF_HEADS = 4
DIFF_HD = 64
CA_HEADS = 16
CA_HD = 64
CA_LEFT_CHUNKS = 8
REL_CLIP = 256
D_FF = 2816
MASKED = -1e30
VMEM_LIMIT = 56 * 1024 * 1024
LANES = 128
BAND_SUB = 128
BAND_WIN = BAND_SUB + CA_LEFT_CHUNKS * CHUNK


def _params(*sem):
    return pltpu.CompilerParams(dimension_semantics=sem, vmem_limit_bytes=VMEM_LIMIT)


def _rms(x, g):
    ms = jnp.mean(x * x, axis=-1, keepdims=True)
    return x * lax.rsqrt(ms + RMS_EPS) * g


def _silu(x):
    return x * jax.nn.sigmoid(x)


def _dot(a, b):
    return jnp.dot(a, b, preferred_element_type=F32)


def _dot_nt(a, b):
    return lax.dot_general(a, b, (((1,), (1,)), ((), ())), preferred_element_type=F32)


def _dot_tn(a, b):
    return lax.dot_general(a, b, (((0,), (0,)), ((), ())), preferred_element_type=F32)


def _dot_exact_lhs(l_bf16, x):
    x1 = x.astype(BF16)
    x2 = (x - x1.astype(F32)).astype(BF16)
    return _dot(l_bf16, x1) + _dot(l_bf16, x2)


def _resident(shape):
    return pl.BlockSpec(shape, lambda *_: (0,) * len(shape), pipeline_mode=pl.Buffered(1))


def _norm_matmul_kernel(x_ref, g_ref, w_ref, *rest, tn):
    xn = _rms(x_ref[...], g_ref[...])
    hi = xn.astype(BF16)
    if len(rest) == 1:
        (o_ref,) = rest
    else:
        whi_ref, wlo_ref, o_ref, o2_ref = rest
        lo = (xn - hi.astype(F32)).astype(BF16)
        o2_ref[...] = _dot(hi, whi_ref[...]) + (_dot(hi, wlo_ref[...]) + _dot(lo, whi_ref[...]))
    for n0 in range(0, w_ref.shape[1], tn):
        o_ref[:, n0:n0 + tn] = _dot(hi, w_ref[:, n0:n0 + tn]).astype(o_ref.dtype)


def _norm_matmul(x, g, w, tm, tn, w_hp=None):
    t, d = x.shape
    n = w.shape[1]
    in_specs = [pl.BlockSpec((tm, d), lambda i: (i, 0)), _resident((1, d)), _resident(w.shape)]
    out_specs = [pl.BlockSpec((tm, n), lambda i: (i, 0))]
    out_shape = [jax.ShapeDtypeStruct((t, n), BF16)]
    args = [x, g, w]
    if w_hp is not None:
        n2 = w_hp[0].shape[1]
        in_specs += [_resident(w_hp[0].shape), _resident(w_hp[1].shape)]
        out_specs.append(pl.BlockSpec((tm, n2), lambda i: (i, 0)))
        out_shape.append(jax.ShapeDtypeStruct((t, n2), F32))
        args += list(w_hp)
    return pl.pallas_call(
        functools.partial(_norm_matmul_kernel, tn=tn),
        grid=(t // tm,),
        in_specs=in_specs,
        out_specs=out_specs,
        out_shape=out_shape,
        compiler_params=_params("parallel"),
        name="norm_matmul",
    )(*args)


def _mix_ffn_kernel(*refs, trans, tm, tf, tiles_per_batch, final_norm):
    n_in = len(trans)
    a_refs = refs[:n_in]
    w_refs = refs[n_in:2 * n_in]
    x_ref, g_ref, wi_ref, cw_ref, wo_ref, fg_ref, o_ref, gs_ref, act_ref = refs[2 * n_in:]
    i = pl.program_id(0)

    @pl.when(i % tiles_per_batch == 0)
    def _():
        gs_ref[0:8, :] = jnp.zeros((8, D_FF), F32)

    x = x_ref[...]
    for a_ref, w_ref, tr in zip(a_refs, w_refs, trans):
        x = x + (_dot_tn if tr else _dot)(a_ref[...], w_ref[...])
    xn = _rms(x, g_ref[...]).astype(BF16)
    for c0 in range(0, D_FF, tf):
        cs = slice(c0, c0 + tf)
        gate = _dot(xn, wi_ref[:, cs])
        up = _dot(xn, wi_ref[:, D_FF + c0:D_FF + c0 + tf])
        gs_ref[8:, cs] = gate
        cw = cw_ref[:, cs]
        y = (cw[2:3] * gate + cw[1:2] * gs_ref[pl.ds(7, tm), cs]
             + cw[0:1] * gs_ref[pl.ds(6, tm), cs] + cw[3:4])
        act_ref[:, cs] = (_silu(y) * up).astype(BF16)
    gs_ref[0:8, :] = gs_ref[tm:tm + 8, :]
    r = x + _dot(act_ref[...], wo_ref[...])
    o_ref[...] = _rms(r, fg_ref[...]) if final_norm else r


def _mix_ffn(acts, ws, x, g, w_in, cwb, w_out, final_g, seq, tm, tf, final_norm):
    t, d = x.shape
    tpb = seq // tm
    trans = tuple(a.ndim == 3 for a in acts)
    kern = functools.partial(_mix_ffn_kernel, trans=trans, tm=tm, tf=tf, tiles_per_batch=tpb,
                             final_norm=final_norm)
    in_specs = ([pl.BlockSpec((None, a.shape[1], tm), lambda i: (i // tpb, 0, i % tpb)) if tr
                 else pl.BlockSpec((tm, a.shape[1]), lambda i: (i, 0)) for a, tr in zip(acts, trans)]
                + [_resident(w.shape) for w in ws]
                + [pl.BlockSpec((tm, d), lambda i: (i, 0)),
                   _resident((1, d)),
                   _resident(w_in.shape),
                   _resident(cwb.shape),
                   _resident(w_out.shape),
                   _resident((1, d))])
    return pl.pallas_call(
        kern,
        grid=(t // tm,),
        in_specs=in_specs,
        out_specs=pl.BlockSpec((tm, d), lambda i: (i, 0)),
        out_shape=jax.ShapeDtypeStruct((t, d), F32),
        scratch_shapes=[pltpu.VMEM((tm + 8, D_FF), F32),
                        pltpu.VMEM((tm, D_FF), BF16)],
        compiler_params=_params("arbitrary"),
        name="mix_ffn",
    )(*acts, *ws, x, g, w_in, cwb, w_out, final_g)


def _rope_kernel(q_ref, k_ref, v_ref, cos_ref, sin_ref, qa_ref, qb_ref, kr_ref, vt_ref):
    cos = cos_ref[...]
    sin = sin_ref[...]
    lane = lax.broadcasted_iota(jnp.int32, cos.shape, 1)
    first = (lane % DIFF_HD) < (DIFF_HD // 2)

    def rope(x):
        rot = jnp.where(first, pltpu.roll(x, LANES - DIFF_HD // 2, 1), pltpu.roll(x, DIFF_HD // 2, 1))
        return x * cos + rot * sin

    lo = lane < DIFF_HD
    for h in range(DIFF_HEADS):
        hs = slice(h * LANES, (h + 1) * LANES)
        q = rope(q_ref[:, hs].astype(F32)) * (DIFF_HD ** -0.5 * math.log2(math.e))
        k = rope(k_ref[:, hs].astype(F32))
        qa_ref[:, hs] = jnp.where(lo, q, 0.0).astype(BF16)
        qb_ref[:, hs] = jnp.where(lo, 0.0, q).astype(BF16)
        kr_ref[:, hs] = k.astype(BF16)
        vt_ref[h] = v_ref[:, hs].astype(F32).T.astype(BF16)


def _rope(p0, cos_t, sin_t, batch, seq, tm, q_blk, k_blk, v_blk):
    t = p0.shape[0]
    w = DIFF_HEADS * LANES
    out = jax.ShapeDtypeStruct((t, w), BF16)
    spt = seq // tm
    return pl.pallas_call(
        _rope_kernel,
        grid=(t // tm,),
        in_specs=[pl.BlockSpec((tm, w), lambda i: (i, q_blk)),
                  pl.BlockSpec((tm, w), lambda i: (i, k_blk)),
                  pl.BlockSpec((tm, w), lambda i: (i, v_blk)),
                  pl.BlockSpec((tm, LANES), lambda i: (i % spt, 0)),
                  pl.BlockSpec((tm, LANES), lambda i: (i % spt, 0))],
        out_specs=[pl.BlockSpec((tm, w), lambda i: (i, 0))] * 3
        + [pl.BlockSpec((None, DIFF_HEADS, None, LANES, tm), lambda i: (i // spt, 0, i % spt, 0, 0))],
        out_shape=[out, out, out, jax.ShapeDtypeStruct((batch, DIFF_HEADS, spt, LANES, tm), BF16)],
        compiler_params=_params("parallel"),
        name="rope",
    )(p0, p0, p0, cos_t, sin_t)


def _diff_attn_kernel(qa_ref, qb_ref, k_ref, vt_ref, lam_ref, gain_ref, o_ref, acc_ref, *, tq, tk):
    qi = pl.program_id(2)
    acc_ref[...] = jnp.zeros(acc_ref.shape, F32)
    qs = (qa_ref[...], qb_ref[...])

    n_sub = tq // tk

    def step(ki, carry, mask, q0=0):
        off = pl.multiple_of(ki * tk, tk)
        kt = k_ref[pl.ds(off, tk), :]
        vt = vt_ref[ki]
        ss = [_dot_nt(kt, qs[mp][q0:]) for mp in range(2)]
        new = []
        for mp in range(2):
            m_old, l_old = carry[2 * mp], carry[2 * mp + 1]
            s = ss[mp]
            if mask is not None:
                s = jnp.where(mask[:, q0:], s, MASKED)
            m_new = jnp.maximum(m_old[:, q0:], jnp.max(s, axis=0, keepdims=True))
            alpha = jnp.exp2(m_old[:, q0:] - m_new)
            p = jnp.exp2(s - m_new)
            l_new = alpha * l_old[:, q0:] + jnp.sum(p, axis=0, keepdims=True)
            acc_ref[mp, :, q0:] = alpha * acc_ref[mp, :, q0:] + _dot(vt, p.astype(BF16))
            if q0:
                m_new = jnp.concatenate([m_old[:, :q0], m_new], axis=1)
                l_new = jnp.concatenate([l_old[:, :q0], l_new], axis=1)
            new += [m_new, l_new]
        return tuple(new)

    row0 = jnp.full((1, tq), MASKED, F32)
    zero = jnp.zeros((1, tq), F32)

    def body(kj, c):
        for sub in range(n_sub):
            c = step(kj * n_sub + sub, c, None)
        return c

    carry = lax.fori_loop(0, qi, body, (row0, zero, row0, zero))

    key = lax.broadcasted_iota(jnp.int32, (tk, tq), 0)
    qry = lax.broadcasted_iota(jnp.int32, (tk, tq), 1)
    for sub in range(n_sub):
        carry = step(qi * n_sub + sub, carry, (key // CHUNK + sub * (tk // CHUNK)) <= (qry // CHUNK), sub * tk)
    _, l1, _, l2 = carry

    o = acc_ref[0] / l1 - lam_ref[...] * (acc_ref[1] / l2)
    ms = jnp.mean(o * o, axis=0, keepdims=True)
    o_ref[...] = (o * lax.rsqrt(ms + RMS_EPS) * gain_ref[...]).astype(o_ref.dtype)


def _diff_attn(qa, qb, kr, vt, lam_row, gain_t, batch, seq, tq):
    nq = seq // tq
    nk, _, tk = vt.shape[2:]
    kern = functools.partial(_diff_attn_kernel, tq=tq, tk=tk)
    qspec = pl.BlockSpec((tq, LANES), lambda b, h, q: (b * nq + q, h))
    return pl.pallas_call(
        kern,
        grid=(batch, DIFF_HEADS, nq),
        in_specs=[qspec, qspec,
                  pl.BlockSpec((seq, LANES), lambda b, h, q: (b, h)),
                  pl.BlockSpec((None, None, nk, LANES, tk), lambda b, h, q: (b, h, 0, 0, 0)),
                  pl.BlockSpec((1, tq), lambda b, h, q: (0, 0)),
                  pl.BlockSpec((LANES, tq), lambda b, h, q: (0, 0))],
        out_specs=pl.BlockSpec((None, LANES, tq), lambda b, h, q: (b, h, q)),
        out_shape=jax.ShapeDtypeStruct((batch, DIFF_HEADS * LANES, seq), BF16),
        scratch_shapes=[pltpu.VMEM((2, LANES, tq), F32)],
        compiler_params=_params("parallel", "parallel", "arbitrary"),
        name="diff_attn",
    )(qa, qb, kr, vt, lam_row, gain_t)


def _band_attn_kernel(q_ref, kc_ref, kp_ref, vc_ref, vp_ref, bias_ref, o_ref,
                      kw_ref, vw_ref, *, tq, tiles_per_batch):
    i = pl.program_id(0)
    pad = CA_LEFT_CHUNKS * CHUNK
    kw_ref[0:tq, :] = kp_ref[...]
    kw_ref[tq:, :] = kc_ref[...]
    vw_ref[0:tq, :] = vp_ref[...]
    vw_ref[tq:, :] = vc_ref[...]
    has_prev = (i % tiles_per_batch) != 0
    lane = lax.broadcasted_iota(jnp.int32, (BAND_SUB, LANES), 1)
    lo = lane < CA_HD
    col = lax.broadcasted_iota(jnp.int32, (BAND_SUB, BAND_WIN), 1)

    def sub_tile(s, carry, first):
        r0 = pl.multiple_of(s * BAND_SUB, BAND_SUB)
        w0 = pl.multiple_of(r0 + (tq - pad), BAND_SUB)
        css = [slice(pr * LANES, (pr + 1) * LANES) for pr in range(CA_HEADS // 2)]
        scs = []
        for pr, cs in enumerate(css):
            qp = q_ref[pl.ds(r0, BAND_SUB), cs]
            kw = kw_ref[pl.ds(w0, BAND_WIN), cs]
            for hh in range(2):
                qh = jnp.where(lo, qp, 0) if hh == 0 else jnp.where(lo, 0, qp)
                scs.append(_dot_nt(qh.astype(BF16), kw))
        ps, dens = [], []
        for h, sc in enumerate(scs):
            sc = sc + bias_ref[h]
            if first:
                sc = jnp.where(col + w0 >= tq, sc, MASKED)
            p = jnp.exp2(sc - jnp.max(sc, axis=-1, keepdims=True))
            dens.append(jnp.sum(p, axis=-1, keepdims=True))
            ps.append(p.astype(BF16))
        for pr, cs in enumerate(css):
            vw = vw_ref[pl.ds(w0, BAND_WIN), cs]
            outs = [_dot(ps[2 * pr + hh], vw) / dens[2 * pr + hh] for hh in range(2)]
            o_ref[pl.ds(r0, BAND_SUB), cs] = jnp.where(lo, outs[0], outs[1]).astype(o_ref.dtype)
        return carry

    @pl.when(has_prev)
    def _():
        lax.fori_loop(0, tq // BAND_SUB, lambda s, c: sub_tile(s, c, False), 0)

    @pl.when(jnp.logical_not(has_prev))
    def _():
        lax.fori_loop(0, tq // BAND_SUB, lambda s, c: sub_tile(s, c, True), 0)


def _band_attn(qkv, bias, seq, tq):
    t = qkv.shape[0]
    d = CA_HEADS * CA_HD
    tpb = seq // tq
    kern = functools.partial(_band_attn_kernel, tq=tq, tiles_per_batch=tpb)

    def prev(i):
        return jnp.where(i % tpb == 0, i, i - 1)

    return pl.pallas_call(
        kern,
        grid=(t // tq,),
        in_specs=[pl.BlockSpec((tq, d), lambda i: (i, 0)),
                  pl.BlockSpec((tq, d), lambda i: (i, 1)),
                  pl.BlockSpec((tq, d), lambda i: (prev(i), 1)),
                  pl.BlockSpec((tq, d), lambda i: (i, 2)),
                  pl.BlockSpec((tq, d), lambda i: (prev(i), 2)),
                  pl.BlockSpec(bias.shape, lambda i: (0, 0, 0))],
        out_specs=pl.BlockSpec((tq, d), lambda i: (i, 0)),
        out_shape=jax.ShapeDtypeStruct((t, d), BF16),
        scratch_shapes=[pltpu.VMEM((2 * tq, d), BF16),
                        pltpu.VMEM((2 * tq, d), BF16)],
        compiler_params=_params("parallel"),
        name="band_attn",
    )(qkv, qkv, qkv, qkv, qkv, bias)


def _deltanet_kernel(q_ref, k_ref, v_ref, qp_ref, kp_ref, vp_ref, ab_ref, gate_ref,
                     cw_ref, ea_ref, dtb_ref, ng_ref, o_ref, state_ref, xx_ref, *, tm):
    i = pl.program_id(1)
    pair = 2 * CHUNK

    @pl.when(i == 0)
    def _():
        state_ref[...] = jnp.zeros(state_ref.shape, F32)

    keep = (i > 0).astype(F32)
    hist = 16

    def conv_silu(cur_ref, prev_ref, c0):
        xx_ref[0:hist, :] = prev_ref[...].astype(F32) * keep
        cur = cur_ref[...].astype(F32)
        xx_ref[hist:, :] = cur
        w = cw_ref[:, c0:c0 + DN_QK]
        y = w[DN_CONV - 1:DN_CONV] * cur
        for s in range(1, DN_CONV):
            y = y + w[DN_CONV - 1 - s:DN_CONV - s] * xx_ref[pl.ds(hist - s, tm), :]
        return _silu(y)

    qc = conv_silu(q_ref, qp_ref, 0)
    kc = conv_silu(k_ref, kp_ref, DN_QK)
    vc = conv_silu(v_ref, vp_ref, 2 * DN_QK)

    ab = ab_ref[...]
    g_all = -ea_ref[...] * jax.nn.softplus(ab + dtb_ref[...])
    beta_all = jax.nn.sigmoid(ab)

    row = lax.broadcasted_iota(jnp.int32, (pair, pair), 0)
    col = lax.broadcasted_iota(jnp.int32, (pair, pair), 1)
    same = (row // CHUNK) == (col // CHUNK)
    tri = jnp.logical_and(same, col <= row)
    strict = jnp.logical_and(same, col < row)
    lmat = jnp.where(tri, 1.0, 0.0).astype(BF16)
    su = jnp.where(jnp.logical_and(same, row > col), 1.0, 0.0)
    eye = jnp.where(row == col, 1.0, 0.0)
    top = row < CHUNK

    npair = tm // pair
    units = [(h, dc) for h in range(DN_HEADS) for dc in range(npair)]
    nu = len(units)
    hsl = [slice(h * DN_DK, (h + 1) * DN_DK) for h in range(DN_HEADS)]
    rsl = [slice(dc * pair, (dc + 1) * pair) for dc in range(npair)]

    def bf(x):
        return x.astype(BF16)

    qn, kn, g_b, beta_b = [], [], [], []
    for h in range(DN_HEADS):
        qh = qc[:, hsl[h]]
        kh = kc[:, hsl[h]]
        qn.append(qh * lax.rsqrt(jnp.sum(qh * qh, axis=-1, keepdims=True) + RMS_EPS) * (DN_DK ** -0.5))
        kn.append(kh * lax.rsqrt(jnp.sum(kh * kh, axis=-1, keepdims=True) + RMS_EPS))
        g_b.append(jnp.broadcast_to(g_all[:, h:h + 1], (tm, pair)))
        beta_b.append(jnp.broadcast_to(beta_all[:, DN_HEADS + h:DN_HEADS + h + 1], (tm, pair)))
    q2 = [qn[h][rsl[dc]] for h, dc in units]
    k2 = [kn[h][rsl[dc]] for h, dc in units]
    v2 = [vc[rsl[dc], hsl[h]] for h, dc in units]
    b2 = [beta_b[h][rsl[dc]] for h, dc in units]
    k2b = [bf(x) for x in k2]

    g_cat = jnp.concatenate([g_b[h][rsl[dc]] for h, dc in units], axis=1)
    gcum_cat = _dot_exact_lhs(lmat, g_cat)
    gdiff_cat = _dot_exact_lhs(lmat, g_cat * jnp.concatenate([su] * nu, axis=1))
    usl = [slice(u * pair, (u + 1) * pair) for u in range(nu)]
    gcum = [gcum_cat[:, s] for s in usl]
    decay = [jnp.where(tri, jnp.exp(jnp.where(tri, gdiff_cat[:, s], 0.0)), 0.0) for s in usl]
    kk = [_dot_nt(x, x) for x in k2b]
    a = [jnp.where(strict, b2[u] * kk[u] * decay[u], 0.0) for u in range(nu)]
    tinv = [eye - x for x in a]
    pw = a
    for _ in range(5):
        pwb = [bf(x) for x in pw]
        pw = [_dot(x, x) for x in pwb]
        pwb = [bf(x) for x in pw]
        tinv = [tinv[u] + _dot(bf(tinv[u]), pwb[u]) for u in range(nu)]
    eg = [jnp.exp(x) for x in gcum]
    rhs = [jnp.concatenate([v2[u] * b2[u], k2[u] * b2[u] * eg[u]], axis=1) for u in range(nu)]
    uw = [_dot(bf(tinv[u]), bf(rhs[u])) for u in range(nu)]
    qk = [jnp.where(tri, _dot_nt(bf(q2[u]), k2b[u]) * decay[u], 0.0) for u in range(nu)]
    qd = [bf(q2[u] * eg[u]) for u in range(nu)]
    gl = [(gcum[u][CHUNK - 1:CHUNK], gcum[u][pair - 1:pair]) for u in range(nu)]
    kd = [bf(k2[u] * jnp.exp(jnp.where(top, gl[u][0], gl[u][1]) - gcum[u])) for u in range(nu)]
    egl = [(jnp.exp(gl[u][0]), jnp.exp(gl[u][1])) for u in range(nu)]

    st = [state_ref[h] for h in range(DN_HEADS)]
    vns = [[None, None] for _ in range(nu)]
    ois = [[None, None] for _ in range(nu)]
    for dc in range(npair):
        for c in range(2):
            cr = slice(c * CHUNK, (c + 1) * CHUNK)
            us = [h * npair + dc for h in range(DN_HEADS)]
            stb = [bf(x) for x in st]
            vn = [uw[u][cr, :DN_DK] - _dot(bf(uw[u][cr, DN_DK:]), stb[h]) for h, u in enumerate(us)]
            for h, u in enumerate(us):
                ois[u][c] = _dot(qd[u][cr], stb[h])
                vns[u][c] = vn[h]
            st = [st[h] * egl[u][c] + _dot_tn(kd[u][cr], bf(vn[h])) for h, u in enumerate(us)]
    for h in range(DN_HEADS):
        state_ref[h] = st[h]

    for u, (h, dc) in enumerate(units):
        vn2 = jnp.concatenate(vns[u], axis=0)
        o = jnp.concatenate(ois[u], axis=0) + _dot(bf(qk[u]), bf(vn2))
        o = _rms(o, ng_ref[...]) * _silu(gate_ref[rsl[dc], hsl[h]].astype(F32))
        o_ref[rsl[dc], hsl[h]] = o.astype(o_ref.dtype)


def _deltanet(p0, ab, cwp, ea, dtb, ng, batch, seq, tm):
    t = p0.shape[0]
    w = DN_QK
    nt = seq // tm
    hist = 16
    kern = functools.partial(_deltanet_kernel, tm=tm)

    def cur(cb):
        return pl.BlockSpec((tm, w), lambda b, i: (b * nt + i, cb))

    def prev(cb):
        return pl.BlockSpec((hist, w), lambda b, i: (jnp.maximum((b * nt + i) * (tm // hist) - 1, 0), cb))

    vec = pl.BlockSpec((1, LANES), lambda b, i: (0, 0))
    return pl.pallas_call(
        kern,
        grid=(batch, nt),
        in_specs=[cur(0), cur(1), cur(2), prev(0), prev(1), prev(2),
                  pl.BlockSpec((tm, LANES), lambda b, i: (b * nt + i, 0)),
                  cur(3),
                  pl.BlockSpec(cwp.shape, lambda b, i: (0, 0)),
                  vec, vec, vec],
        out_specs=pl.BlockSpec((tm, w), lambda b, i: (b * nt + i, 0)),
        out_shape=jax.ShapeDtypeStruct((t, w), BF16),
        scratch_shapes=[pltpu.VMEM((DN_HEADS, DN_DK, DN_DK), F32),
                        pltpu.VMEM((tm + hist, w), F32)],
        compiler_params=_params("arbitrary", "arbitrary"),
        name="deltanet",
    )(p0, p0, p0, p0, p0, p0, ab, p0, cwp, ea, dtb, ng)


def _tile(n, want):
    return min(n, want)


def _pad_rows(a, rows):
    return jnp.concatenate([a, jnp.zeros((rows - a.shape[0], a.shape[1]), a.dtype)], axis=0)


def kernel(x, even_norm_mix, even_w_in, even_dn_conv, even_dn_A_log, even_dn_dt_bias, even_dn_norm,
           even_diff_lambda, even_diff_subln, even_w_out, odd_norm_mix, odd_w_qkv, odd_rel_bias,
           odd_w_out, ffn_norm, ffn_w_in, ffn_conv_w, ffn_conv_b, ffn_w_out, final_norm):
    batch, seq, d = x.shape
    t = batch * seq
    xr = x.reshape(t, d)
    tm = _tile(seq, 512)

    def mix_ffn(acts, ws, xr, i, final):
        cwb = _pad_rows(jnp.concatenate([ffn_conv_w[i], ffn_conv_b[i][None, :]], axis=0), 8)
        return _mix_ffn(acts, ws, xr, ffn_norm[i][None, :], ffn_w_in[i].astype(BF16), cwb,
                        ffn_w_out[i].astype(BF16), final_norm[None, :], seq, tm, 256, final)

    w0 = even_w_in[0]
    n_qkv = 3 * DN_QK
    w_main = jnp.concatenate([w0[:, :n_qkv], w0[:, n_qkv + 2 * DN_HEADS:]], axis=1).astype(BF16)
    w_ab = jnp.pad(w0[:, n_qkv:n_qkv + 2 * DN_HEADS], ((0, 0), (0, LANES - 2 * DN_HEADS)))
    w_ab_hi = w_ab.astype(BF16)
    w_ab_lo = (w_ab - w_ab_hi.astype(F32)).astype(BF16)
    g0 = even_norm_mix[0][None, :]
    p0, ab = _norm_matmul(xr, g0, w_main, tm, 512, (w_ab_hi, w_ab_lo))

    def lane_vec(v):
        return jnp.pad(v.astype(F32), (0, LANES - v.shape[0]))[None, :]

    o_a = _deltanet(p0, ab, _pad_rows(even_dn_conv[0], 8), lane_vec(jnp.exp(even_dn_A_log[0])),
                    lane_vec(even_dn_dt_bias[0]), even_dn_norm[0][None, :], batch, seq, _tile(seq, 512))

    half = DIFF_HD // 2
    inv = 1.0 / (ROPE_THETA ** (jnp.arange(0, DIFF_HD, 2, dtype=F32) / DIFF_HD))
    ang = jnp.arange(seq, dtype=F32)[:, None] * inv[None, :]
    cos_t = jnp.tile(jnp.cos(ang), (1, LANES // half))
    sign = jnp.tile(jnp.concatenate([-jnp.ones((half,), F32), jnp.ones((half,), F32)]), LANES // DIFF_HD)
    sin_t = jnp.tile(jnp.sin(ang), (1, LANES // half)) * sign[None, :]
    qa, qb, kr, vt = _rope(p0, cos_t, sin_t, batch, seq, tm, 4, 5, 6)

    lam_init = 0.8 - 0.6 * math.exp(-0.3 * 0)
    lp = even_diff_lambda[0].astype(F32)
    lam = jnp.exp(jnp.sum(lp[0] * lp[1])) - jnp.exp(jnp.sum(lp[2] * lp[3])) + lam_init
    tq = _tile(seq, 1024)
    lam_row = jnp.full((1, tq), lam, F32)
    gain_t = jnp.broadcast_to((even_diff_subln[0] * (1.0 - lam_init))[:, None], (LANES, tq))
    o_b = _diff_attn(qa, qb, kr, vt, lam_row, gain_t, batch, seq, tq)

    wo = even_w_out[0].astype(BF16)
    xr = mix_ffn([o_a, o_b], [wo[:DN_QK], wo[DN_QK:]], xr, 0, False)

    wq = odd_w_qkv[0]
    log2e = math.log2(math.e)
    wq = jnp.concatenate([wq[:, :D_MODEL] * (CA_HD ** -0.5 * log2e), wq[:, D_MODEL:]], axis=1).astype(BF16)
    (qkv,) = _norm_matmul(xr, odd_norm_mix[0][None, :], wq, tm, 512)
    pad = CA_LEFT_CHUNKS * CHUNK
    qi = jnp.arange(BAND_SUB)[:, None]
    kj = jnp.arange(BAND_WIN)[None, :]
    in_band = jnp.logical_and(kj // CHUNK >= qi // CHUNK, kj // CHUNK <= qi // CHUNK + CA_LEFT_CHUNKS)
    m_len = BAND_WIN + BAND_SUB + 1
    rel = jnp.arange(m_len) - BAND_SUB - pad
    e = odd_rel_bias[0].astype(F32)[:, jnp.clip(rel, -REL_CLIP, REL_CLIP) + REL_CLIP]
    bias = jnp.tile(e, (1, BAND_SUB))[:, :BAND_SUB * (m_len - 1)].reshape(CA_HEADS, BAND_SUB, m_len - 1)
    bias = jnp.where(in_band[None], bias[:, :, BAND_SUB:BAND_SUB + BAND_WIN] * log2e, MASKED)
    o_c = _band_attn(qkv, bias, seq, tm)
    xr = mix_ffn([o_c], [odd_w_out[0].astype(BF16)], xr, 1, True)
    return xr.reshape(batch, seq, d)
```

```python
import functools
import math

import jax
import jax.numpy as jnp
from jax import lax
from jax.experimental import pallas as pl
from jax.experimental.pallas import tpu as pltpu

F32 = jnp.float32
BF16 = jnp.bfloat16

D_MODEL = 1024
CHUNK = 64
RMS_EPS = 1e-6
ROPE_THETA = 10000.0
DN_HEADS = 4
DN_DK = 128
DN_QK = 512
DN_CONV = 4
DIFF_HEADS = 4
DIFF_HD = 64
CA_HEADS = 16
CA_HD = 64
CA_LEFT_CHUNKS = 8
REL_CLIP = 256
D_FF = 2816
MASKED = -1e30
VMEM_LIMIT = 56 * 1024 * 1024
LANES = 128
BAND_SUB = 128
BAND_WIN = BAND_SUB + CA_LEFT_CHUNKS * CHUNK

def _params(*sem):
    return pltpu.CompilerParams(dimension_semantics=sem, vmem_limit_bytes=VMEM_LIMIT)


def _rms(x, g):
    ms = jnp.mean(x * x, axis=-1, keepdims=True)
    return x * lax.rsqrt(ms + RMS_EPS) * g


def _silu(x):
    return x * jax.nn.sigmoid(x)


def _dot(a, b):
    return jnp.dot(a, b, preferred_element_type=F32)


def _dot_nt(a, b):
    return lax.dot_general(a, b, (((1,), (1,)), ((), ())), preferred_element_type=F32)


def _dot_tn(a, b):
    return lax.dot_general(a, b, (((0,), (0,)), ((), ())), preferred_element_type=F32)


def _dot_exact_lhs(l_bf16, x):
    x1 = x.astype(BF16)
    x2 = (x - x1.astype(F32)).astype(BF16)
    return _dot(l_bf16, x1) + _dot(l_bf16, x2)


def _resident(shape):
    return pl.BlockSpec(shape, lambda *_: (0,) * len(shape), pipeline_mode=pl.Buffered(1))


def _norm_matmul_kernel(x_ref, g_ref, w_ref, *rest, tn):
    xn = _rms(x_ref[...], g_ref[...])
    hi = xn.astype(BF16)
    if len(rest) == 1:
        (o_ref,) = rest
    else:
        whi_ref, wlo_ref, o_ref, o2_ref = rest
        lo = (xn - hi.astype(F32)).astype(BF16)
        o2_ref[...] = _dot(hi, whi_ref[...]) + (_dot(hi, wlo_ref[...]) + _dot(lo, whi_ref[...]))
    for n0 in range(0, w_ref.shape[1], tn):
        o_ref[:, n0:n0 + tn] = _dot(hi, w_ref[:, n0:n0 + tn]).astype(o_ref.dtype)


def _norm_matmul(x, g, w, tm, tn, w_hp=None):
    t, d = x.shape
    n = w.shape[1]
    in_specs = [pl.BlockSpec((tm, d), lambda i: (i, 0)), _resident((1, d)), _resident(w.shape)]
    out_specs = [pl.BlockSpec((tm, n), lambda i: (i, 0))]
    out_shape = [jax.ShapeDtypeStruct((t, n), BF16)]
    args = [x, g, w]
    if w_hp is not None:
        n2 = w_hp[0].shape[1]
        in_specs += [_resident(w_hp[0].shape), _resident(w_hp[1].shape)]
        out_specs.append(pl.BlockSpec((tm, n2), lambda i: (i, 0)))
        out_shape.append(jax.ShapeDtypeStruct((t, n2), F32))
        args += list(w_hp)
    return pl.pallas_call(
        functools.partial(_norm_matmul_kernel, tn=tn),
        grid=(t // tm,),
        in_specs=in_specs,
        out_specs=out_specs,
        out_shape=out_shape,
        compiler_params=_params("parallel"),
        name="norm_matmul",
    )(*args)


def _mix_ffn_kernel(*refs, trans, tm, tf, tiles_per_batch, final_norm):
    n_in = len(trans)
    a_refs = refs[:n_in]
    w_refs = refs[n_in:2 * n_in]
    x_ref, g_ref, wi_ref, cw_ref, wo_ref, fg_ref, o_ref, gs_ref, act_ref = refs[2 * n_in:]
    i = pl.program_id(0)

    @pl.when(i % tiles_per_batch == 0)
    def _():
        gs_ref[0:8, :] = jnp.zeros((8, D_FF), F32)

    x = x_ref[...]
    for a_ref, w_ref, tr in zip(a_refs, w_refs, trans):
        x = x + (_dot_tn if tr else _dot)(a_ref[...], w_ref[...])
    xn = _rms(x, g_ref[...]).astype(BF16)
    for c0 in range(0, D_FF, tf):
        cs = slice(c0, c0 + tf)
        gate = _dot(xn, wi_ref[:, cs])
        up = _dot(xn, wi_ref[:, D_FF + c0:D_FF + c0 + tf])
        gs_ref[8:, cs] = gate
        cw = cw_ref[:, cs]
        y = (cw[2:3] * gate + cw[1:2] * gs_ref[pl.ds(7, tm), cs]
             + cw[0:1] * gs_ref[pl.ds(6, tm), cs] + cw[3:4])
        act_ref[:, cs] = (_silu(y) * up).astype(BF16)
    gs_ref[0:8, :] = gs_ref[tm:tm + 8, :]
    r = x + _dot(act_ref[...], wo_ref[...])
    o_ref[...] = _rms(r, fg_ref[...]) if final_norm else r


def _mix_ffn(acts, ws, x, g, w_in, cwb, w_out, final_g, seq, tm, tf, final_norm):
    t, d = x.shape
    tpb = seq // tm
    trans = tuple(a.ndim == 3 for a in acts)
    kern = functools.partial(_mix_ffn_kernel, trans=trans, tm=tm, tf=tf, tiles_per_batch=tpb,
                             final_norm=final_norm)
    in_specs = ([pl.BlockSpec((None, a.shape[1], tm), lambda i: (i // tpb, 0, i % tpb)) if tr
                 else pl.BlockSpec((tm, a.shape[1]), lambda i: (i, 0)) for a, tr in zip(acts, trans)]
                + [_resident(w.shape) for w in ws]
                + [pl.BlockSpec((tm, d), lambda i: (i, 0)),
                   _resident((1, d)),
                   _resident(w_in.shape),
                   _resident(cwb.shape),
                   _resident(w_out.shape),
                   _resident((1, d))])
    return pl.pallas_call(
        kern,
        grid=(t // tm,),
        in_specs=in_specs,
        out_specs=pl.BlockSpec((tm, d), lambda i: (i, 0)),
        out_shape=jax.ShapeDtypeStruct((t, d), F32),
        scratch_shapes=[pltpu.VMEM((tm + 8, D_FF), F32),
                        pltpu.VMEM((tm, D_FF), BF16)],
        compiler_params=_params("arbitrary"),
        name="mix_ffn",
    )(*acts, *ws, x, g, w_in, cwb, w_out, final_g)


def _rope_kernel(q_ref, k_ref, v_ref, cos_ref, sin_ref, qa_ref, qb_ref, kr_ref, vt_ref):
    cos = cos_ref[...]
    sin = sin_ref[...]
    lane = lax.broadcasted_iota(jnp.int32, cos.shape, 1)
    first = (lane % DIFF_HD) < (DIFF_HD // 2)

    def rope(x):
        rot = jnp.where(first, pltpu.roll(x, LANES - DIFF_HD // 2, 1), pltpu.roll(x, DIFF_HD // 2, 1))
        return x * cos + rot * sin

    lo = lane < DIFF_HD
    for h in range(DIFF_HEADS):
        hs = slice(h * LANES, (h + 1) * LANES)
        q = rope(q_ref[:, hs].astype(F32)) * (DIFF_HD ** -0.5 * math.log2(math.e))
        k = rope(k_ref[:, hs].astype(F32))
        qa_ref[:, hs] = jnp.where(lo, q, 0.0).astype(BF16)
        qb_ref[:, hs] = jnp.where(lo, 0.0, q).astype(BF16)
        kr_ref[:, hs] = k.astype(BF16)
        vt_ref[h] = v_ref[:, hs].astype(F32).T.astype(BF16)


def _rope(p0, cos_t, sin_t, batch, seq, tm, q_blk, k_blk, v_blk):
    t = p0.shape[0]
    w = DIFF_HEADS * LANES
    out = jax.ShapeDtypeStruct((t, w), BF16)
    spt = seq // tm
    return pl.pallas_call(
        _rope_kernel,
        grid=(t // tm,),
        in_specs=[pl.BlockSpec((tm, w), lambda i: (i, q_blk)),
                  pl.BlockSpec((tm, w), lambda i: (i, k_blk)),
                  pl.BlockSpec((tm, w), lambda i: (i, v_blk)),
                  pl.BlockSpec((tm, LANES), lambda i: (i % spt, 0)),
                  pl.BlockSpec((tm, LANES), lambda i: (i % spt, 0))],
        out_specs=[pl.BlockSpec((tm, w), lambda i: (i, 0))] * 3
        + [pl.BlockSpec((None, DIFF_HEADS, None, LANES, tm), lambda i: (i // spt, 0, i % spt, 0, 0))],
        out_shape=[out, out, out, jax.ShapeDtypeStruct((batch, DIFF_HEADS, spt, LANES, tm), BF16)],
        compiler_params=_params("parallel"),
        name="rope",
    )(p0, p0, p0, cos_t, sin_t)


def _diff_attn_kernel(qa_ref, qb_ref, k_ref, vt_ref, lam_ref, gain_ref, o_ref, acc_ref, s_ref, *, tq, tk):
    qi = pl.program_id(2)
    acc_ref[...] = jnp.zeros(acc_ref.shape, F32)
    qs = (qa_ref[...], qb_ref[...])
    assert tq == 2 * tk

    def scores(ki, buf, q0=0):
        off = pl.multiple_of(ki * tk, tk)
        kt = k_ref[pl.ds(off, tk), :]
        for mp in range(2):
            s_ref[buf, mp, :, q0:] = _dot_nt(kt, qs[mp][q0:])

    def softmax_pv(ki, buf, carry, mask, q0=0):
        vt = vt_ref[ki]
        new, ps, alphas = [], [], []
        for mp in range(2):
            m_old, l_old = carry[2 * mp], carry[2 * mp + 1]
            s = s_ref[buf, mp, :, q0:]
            if mask is not None:
                s = jnp.where(mask[:, q0:], s, MASKED)
            m_new = jnp.maximum(m_old[:, q0:], jnp.max(s, axis=0, keepdims=True))
            alpha = jnp.exp2(m_old[:, q0:] - m_new)
            p = jnp.exp2(s - m_new)
            l_new = alpha * l_old[:, q0:] + jnp.sum(p, axis=0, keepdims=True)
            ps.append(p.astype(BF16))
            alphas.append(alpha)
            if q0:
                m_new = jnp.concatenate([m_old[:, :q0], m_new], axis=1)
                l_new = jnp.concatenate([l_old[:, :q0], l_new], axis=1)
            new += [m_new, l_new]
        for mp in range(2):
            acc_ref[mp, :, q0:] = alphas[mp] * acc_ref[mp, :, q0:] + _dot(vt, ps[mp])
        return tuple(new)

    row0 = jnp.full((1, tq), MASKED, F32)
    zero = jnp.zeros((1, tq), F32)

    scores(0, 0)

    def body(kj, c):
        scores(2 * kj + 1, 1)
        c = softmax_pv(2 * kj, 0, c, None)
        scores(2 * kj + 2, 0)
        return softmax_pv(2 * kj + 1, 1, c, None)

    carry = lax.fori_loop(0, qi, body, (row0, zero, row0, zero))

    key = lax.broadcasted_iota(jnp.int32, (tk, tq), 0)
    qry = lax.broadcasted_iota(jnp.int32, (tk, tq), 1)
    scores(2 * qi + 1, 1, tk)
    carry = softmax_pv(2 * qi, 0, carry, (key // CHUNK) <= (qry // CHUNK))
    carry = softmax_pv(2 * qi + 1, 1, carry, (key // CHUNK + tk // CHUNK) <= (qry // CHUNK), tk)
    _, l1, _, l2 = carry

    o = acc_ref[0] / l1 - lam_ref[...] * (acc_ref[1] / l2)
    ms = jnp.mean(o * o, axis=0, keepdims=True)
    o_ref[...] = (o * lax.rsqrt(ms + RMS_EPS) * gain_ref[...]).astype(o_ref.dtype)


def _diff_attn(qa, qb, kr, vt, lam_row, gain_t, batch, seq, tq):
    nq = seq // tq
    nk, _, tk = vt.shape[2:]
    kern = functools.partial(_diff_attn_kernel, tq=tq, tk=tk)
    qspec = pl.BlockSpec((tq, LANES), lambda b, h, q: (b * nq + q, h))
    return pl.pallas_call(
        kern,
        grid=(batch, DIFF_HEADS, nq),
        in_specs=[qspec, qspec,
                  pl.BlockSpec((seq, LANES), lambda b, h, q: (b, h)),
                  pl.BlockSpec((None, None, nk, LANES, tk), lambda b, h, q: (b, h, 0, 0, 0)),
                  pl.BlockSpec((1, tq), lambda b, h, q: (0, 0)),
                  pl.BlockSpec((LANES, tq), lambda b, h, q: (0, 0))],
        out_specs=pl.BlockSpec((None, LANES, tq), lambda b, h, q: (b, h, q)),
        out_shape=jax.ShapeDtypeStruct((batch, DIFF_HEADS * LANES, seq), BF16),
        scratch_shapes=[pltpu.VMEM((2, LANES, tq), F32),
                        pltpu.VMEM((2, 2, tk, tq), F32)],
        compiler_params=_params("parallel", "parallel", "arbitrary"),
        name="diff_attn",
    )(qa, qb, kr, vt, lam_row, gain_t)


def _band_attn_kernel(q_ref, kc_ref, kp_ref, vc_ref, vp_ref, bias_ref, o_ref,
                      kw_ref, vw_ref, *, tq, tiles_per_batch):
    i = pl.program_id(0)
    pad = CA_LEFT_CHUNKS * CHUNK
    kw_ref[0:tq, :] = kp_ref[...]
    kw_ref[tq:, :] = kc_ref[...]
    vw_ref[0:tq, :] = vp_ref[...]
    vw_ref[tq:, :] = vc_ref[...]
    has_prev = (i % tiles_per_batch) != 0
    lane = lax.broadcasted_iota(jnp.int32, (BAND_SUB, LANES), 1)
    lo = lane < CA_HD
    col = lax.broadcasted_iota(jnp.int32, (BAND_SUB, BAND_WIN), 1)

    def sub_tile(s, carry, first):
        r0 = pl.multiple_of(s * BAND_SUB, BAND_SUB)
        w0 = pl.multiple_of(r0 + (tq - pad), BAND_SUB)
        css = [slice(pr * LANES, (pr + 1) * LANES) for pr in range(CA_HEADS // 2)]
        scs = []
        for pr, cs in enumerate(css):
            qp = q_ref[pl.ds(r0, BAND_SUB), cs]
            kw = kw_ref[pl.ds(w0, BAND_WIN), cs]
            for hh in range(2):
                qh = jnp.where(lo, qp, 0) if hh == 0 else jnp.where(lo, 0, qp)
                scs.append(_dot_nt(qh.astype(BF16), kw))
        ps, dens = [], []
        for h, sc in enumerate(scs):
            sc = sc + bias_ref[h]
            if first:
                sc = jnp.where(col + w0 >= tq, sc, MASKED)
            p = jnp.exp2(sc - jnp.max(sc, axis=-1, keepdims=True))
            dens.append(jnp.sum(p, axis=-1, keepdims=True))
            ps.append(p.astype(BF16))
        for pr, cs in enumerate(css):
            vw = vw_ref[pl.ds(w0, BAND_WIN), cs]
            outs = [_dot(ps[2 * pr + hh], vw) / dens[2 * pr + hh] for hh in range(2)]
            o_ref[pl.ds(r0, BAND_SUB), cs] = jnp.where(lo, outs[0], outs[1]).astype(o_ref.dtype)
        return carry

    @pl.when(has_prev)
    def _():
        lax.fori_loop(0, tq // BAND_SUB, lambda s, c: sub_tile(s, c, False), 0)

    @pl.when(jnp.logical_not(has_prev))
    def _():
        lax.fori_loop(0, tq // BAND_SUB, lambda s, c: sub_tile(s, c, True), 0)


def _band_attn(qkv, bias, seq, tq):
    t = qkv.shape[0]
    d = CA_HEADS * CA_HD
    tpb = seq // tq
    kern = functools.partial(_band_attn_kernel, tq=tq, tiles_per_batch=tpb)

    def prev(i):
        return jnp.where(i % tpb == 0, i, i - 1)

    return pl.pallas_call(
        kern,
        grid=(t // tq,),
        in_specs=[pl.BlockSpec((tq, d), lambda i: (i, 0)),
                  pl.BlockSpec((tq, d), lambda i: (i, 1)),
                  pl.BlockSpec((tq, d), lambda i: (prev(i), 1)),
                  pl.BlockSpec((tq, d), lambda i: (i, 2)),
                  pl.BlockSpec((tq, d), lambda i: (prev(i), 2)),
                  pl.BlockSpec(bias.shape, lambda i: (0, 0, 0))],
        out_specs=pl.BlockSpec((tq, d), lambda i: (i, 0)),
        out_shape=jax.ShapeDtypeStruct((t, d), BF16),
        scratch_shapes=[pltpu.VMEM((2 * tq, d), BF16),
                        pltpu.VMEM((2 * tq, d), BF16)],
        compiler_params=_params("parallel"),
        name="band_attn",
    )(qkv, qkv, qkv, qkv, qkv, bias)


def _deltanet_kernel(q_ref, k_ref, v_ref, qp_ref, kp_ref, vp_ref, ab_ref, gate_ref,
                     cw_ref, ea_ref, dtb_ref, ng_ref, o_ref, state_ref, xx_ref, *, tm):
    i = pl.program_id(1)
    pair = 2 * CHUNK

    @pl.when(i == 0)
    def _():
        state_ref[...] = jnp.zeros(state_ref.shape, F32)

    keep = (i > 0).astype(F32)
    hist = 16

    def conv_silu(cur_ref, prev_ref, c0):
        xx_ref[0:hist, :] = prev_ref[...].astype(F32) * keep
        cur = cur_ref[...].astype(F32)
        xx_ref[hist:, :] = cur
        w = cw_ref[:, c0:c0 + DN_QK]
        y = w[DN_CONV - 1:DN_CONV] * cur
        for s in range(1, DN_CONV):
            y = y + w[DN_CONV - 1 - s:DN_CONV - s] * xx_ref[pl.ds(hist - s, tm), :]
        return _silu(y)

    qc = conv_silu(q_ref, qp_ref, 0)
    kc = conv_silu(k_ref, kp_ref, DN_QK)
    vc = conv_silu(v_ref, vp_ref, 2 * DN_QK)

    ab = ab_ref[...]
    g_all = -ea_ref[...] * jax.nn.softplus(ab + dtb_ref[...])
    beta_all = jax.nn.sigmoid(ab)

    row = lax.broadcasted_iota(jnp.int32, (pair, pair), 0)
    col = lax.broadcasted_iota(jnp.int32, (pair, pair), 1)
    same = (row // CHUNK) == (col // CHUNK)
    tri = jnp.logical_and(same, col <= row)
    strict = jnp.logical_and(same, col < row)
    lmat = jnp.where(tri, 1.0, 0.0).astype(BF16)
    su = jnp.where(jnp.logical_and(same, row > col), 1.0, 0.0)
    eye = jnp.where(row == col, 1.0, 0.0)
    top = row < CHUNK

    npair = tm // pair
    units = [(h, dc) for h in range(DN_HEADS) for dc in range(npair)]
    nu = len(units)
    hsl = [slice(h * DN_DK, (h + 1) * DN_DK) for h in range(DN_HEADS)]
    rsl = [slice(dc * pair, (dc + 1) * pair) for dc in range(npair)]

    def bf(x):
        return x.astype(BF16)

    qn, kn, g_b, beta_b = [], [], [], []
    for h in range(DN_HEADS):
        qh = qc[:, hsl[h]]
        kh = kc[:, hsl[h]]
        qn.append(qh * lax.rsqrt(jnp.sum(qh * qh, axis=-1, keepdims=True) + RMS_EPS) * (DN_DK ** -0.5))
        kn.append(kh * lax.rsqrt(jnp.sum(kh * kh, axis=-1, keepdims=True) + RMS_EPS))
        g_b.append(jnp.broadcast_to(g_all[:, h:h + 1], (tm, pair)))
        beta_b.append(jnp.broadcast_to(beta_all[:, DN_HEADS + h:DN_HEADS + h + 1], (tm, pair)))
    q2 = [qn[h][rsl[dc]] for h, dc in units]
    k2 = [kn[h][rsl[dc]] for h, dc in units]
    v2 = [vc[rsl[dc], hsl[h]] for h, dc in units]
    b2 = [beta_b[h][rsl[dc]] for h, dc in units]
    k2b = [bf(x) for x in k2]

    g_cat = jnp.concatenate([g_b[h][rsl[dc]] for h, dc in units], axis=1)
    gcum_cat = _dot_exact_lhs(lmat, g_cat)
    gdiff_cat = _dot_exact_lhs(lmat, g_cat * jnp.concatenate([su] * nu, axis=1))
    usl = [slice(u * pair, (u + 1) * pair) for u in range(nu)]
    gcum = [gcum_cat[:, s] for s in usl]
    decay = [jnp.where(tri, jnp.exp(jnp.where(tri, gdiff_cat[:, s], 0.0)), 0.0) for s in usl]
    kk = [_dot_nt(x, x) for x in k2b]
    a = [jnp.where(strict, b2[u] * kk[u] * decay[u], 0.0) for u in range(nu)]
    tinv = [eye - x for x in a]
    pw = a
    for _ in range(5):
        pwb = [bf(x) for x in pw]
        pw = [_dot(x, x) for x in pwb]
        pwb = [bf(x) for x in pw]
        tinv = [tinv[u] + _dot(bf(tinv[u]), pwb[u]) for u in range(nu)]
    eg = [jnp.exp(x) for x in gcum]
    rhs = [jnp.concatenate([v2[u] * b2[u], k2[u] * b2[u] * eg[u]], axis=1) for u in range(nu)]
    uw = [_dot(bf(tinv[u]), bf(rhs[u])) for u in range(nu)]
    qk = [jnp.where(tri, _dot_nt(bf(q2[u]), k2b[u]) * decay[u], 0.0) for u in range(nu)]
    qd = [bf(q2[u] * eg[u]) for u in range(nu)]
    gl = [(gcum[u][CHUNK - 1:CHUNK], gcum[u][pair - 1:pair]) for u in range(nu)]
    kd = [bf(k2[u] * jnp.exp(jnp.where(top, gl[u][0], gl[u][1]) - gcum[u])) for u in range(nu)]
    egl = [(jnp.exp(gl[u][0]), jnp.exp(gl[u][1])) for u in range(nu)]

    st = [state_ref[h] for h in range(DN_HEADS)]
    vns = [[None, None] for _ in range(nu)]
    ois = [[None, None] for _ in range(nu)]
    for dc in range(npair):
        for c in range(2):
            cr = slice(c * CHUNK, (c + 1) * CHUNK)
            us = [h * npair + dc for h in range(DN_HEADS)]
            stb = [bf(x) for x in st]
            vn = [uw[u][cr, :DN_DK] - _dot(bf(uw[u][cr, DN_DK:]), stb[h]) for h, u in enumerate(us)]
            for h, u in enumerate(us):
                ois[u][c] = _dot(qd[u][cr], stb[h])
                vns[u][c] = vn[h]
            st = [st[h] * egl[u][c] + _dot_tn(kd[u][cr], bf(vn[h])) for h, u in enumerate(us)]
    for h in range(DN_HEADS):
        state_ref[h] = st[h]

    for u, (h, dc) in enumerate(units):
        vn2 = jnp.concatenate(vns[u], axis=0)
        o = jnp.concatenate(ois[u], axis=0) + _dot(bf(qk[u]), bf(vn2))
        o = _rms(o, ng_ref[...]) * _silu(gate_ref[rsl[dc], hsl[h]].astype(F32))
        o_ref[rsl[dc], hsl[h]] = o.astype(o_ref.dtype)


def _deltanet(p0, ab, cwp, ea, dtb, ng, batch, seq, tm):
    t = p0.shape[0]
    w = DN_QK
    nt = seq // tm
    hist = 16
    kern = functools.partial(_deltanet_kernel, tm=tm)

    def cur(cb):
        return pl.BlockSpec((tm, w), lambda b, i: (b * nt + i, cb))

    def prev(cb):
        return pl.BlockSpec((hist, w), lambda b, i: (jnp.maximum((b * nt + i) * (tm // hist) - 1, 0), cb))

    vec = pl.BlockSpec((1, LANES), lambda b, i: (0, 0))
    return pl.pallas_call(
        kern,
        grid=(batch, nt),
        in_specs=[cur(0), cur(1), cur(2), prev(0), prev(1), prev(2),
                  pl.BlockSpec((tm, LANES), lambda b, i: (b * nt + i, 0)),
                  cur(3),
                  pl.BlockSpec(cwp.shape, lambda b, i: (0, 0)),
                  vec, vec, vec],
        out_specs=pl.BlockSpec((tm, w), lambda b, i: (b * nt + i, 0)),
        out_shape=jax.ShapeDtypeStruct((t, w), BF16),
        scratch_shapes=[pltpu.VMEM((DN_HEADS, DN_DK, DN_DK), F32),
                        pltpu.VMEM((tm + hist, w), F32)],
        compiler_params=_params("arbitrary", "arbitrary"),
        name="deltanet",
    )(p0, p0, p0, p0, p0, p0, ab, p0, cwp, ea, dtb, ng)


def _tile(n, want):
    return min(n, want)


def _pad_rows(a, rows):
    return jnp.concatenate([a, jnp.zeros((rows - a.shape[0], a.shape[1]), a.dtype)], axis=0)


def kernel(x, even_norm_mix, even_w_in, even_dn_conv, even_dn_A_log, even_dn_dt_bias, even_dn_norm,
           even_diff_lambda, even_diff_subln, even_w_out, odd_norm_mix, odd_w_qkv, odd_rel_bias,
           odd_w_out, ffn_norm, ffn_w_in, ffn_conv_w, ffn_conv_b, ffn_w_out, final_norm):
    batch, seq, d = x.shape
    t = batch * seq
    xr = x.reshape(t, d)
    tm = _tile(seq, 512)

    def mix_ffn(acts, ws, xr, i, final):
        cwb = _pad_rows(jnp.concatenate([ffn_conv_w[i], ffn_conv_b[i][None, :]], axis=0), 8)
        return _mix_ffn(acts, ws, xr, ffn_norm[i][None, :], ffn_w_in[i].astype(BF16), cwb,
                        ffn_w_out[i].astype(BF16), final_norm[None, :], seq, tm, 256, final)

    w0 = even_w_in[0]
    n_qkv = 3 * DN_QK
    w_main = jnp.concatenate([w0[:, :n_qkv], w0[:, n_qkv + 2 * DN_HEADS:]], axis=1).astype(BF16)
    w_ab = jnp.pad(w0[:, n_qkv:n_qkv + 2 * DN_HEADS], ((0, 0), (0, LANES - 2 * DN_HEADS)))
    w_ab_hi = w_ab.astype(BF16)
    w_ab_lo = (w_ab - w_ab_hi.astype(F32)).astype(BF16)
    g0 = even_norm_mix[0][None, :]
    p0, ab = _norm_matmul(xr, g0, w_main, tm, 512, (w_ab_hi, w_ab_lo))

    def lane_vec(v):
        return jnp.pad(v.astype(F32), (0, LANES - v.shape[0]))[None, :]

    o_a = _deltanet(p0, ab, _pad_rows(even_dn_conv[0], 8), lane_vec(jnp.exp(even_dn_A_log[0])),
                    lane_vec(even_dn_dt_bias[0]), even_dn_norm[0][None, :], batch, seq, _tile(seq, 512))

    half = DIFF_HD // 2
    inv = 1.0 / (ROPE_THETA ** (jnp.arange(0, DIFF_HD, 2, dtype=F32) / DIFF_HD))
    ang = jnp.arange(seq, dtype=F32)[:, None] * inv[None, :]
    cos_t = jnp.tile(jnp.cos(ang), (1, LANES // half))
    sign = jnp.tile(jnp.concatenate([-jnp.ones((half,), F32), jnp.ones((half,), F32)]), LANES // DIFF_HD)
    sin_t = jnp.tile(jnp.sin(ang), (1, LANES // half)) * sign[None, :]
    qa, qb, kr, vt = _rope(p0, cos_t, sin_t, batch, seq, tm, 4, 5, 6)

    lam_init = 0.8 - 0.6 * math.exp(-0.3 * 0)
    lp = even_diff_lambda[0].astype(F32)
    lam = jnp.exp(jnp.sum(lp[0] * lp[1])) - jnp.exp(jnp.sum(lp[2] * lp[3])) + lam_init
    tq = _tile(seq, 1024)
    lam_row = jnp.full((1, tq), lam, F32)
    gain_t = jnp.broadcast_to((even_diff_subln[0] * (1.0 - lam_init))[:, None], (LANES, tq))
    o_b = _diff_attn(qa, qb, kr, vt, lam_row, gain_t, batch, seq, tq)

    wo = even_w_out[0].astype(BF16)
    xr = mix_ffn([o_a, o_b], [wo[:DN_QK], wo[DN_QK:]], xr, 0, False)

    wq = odd_w_qkv[0]
    log2e = math.log2(math.e)
    wq = jnp.concatenate([wq[:, :D_MODEL] * (CA_HD ** -0.5 * log2e), wq[:, D_MODEL:]], axis=1).astype(BF16)
    (qkv,) = _norm_matmul(xr, odd_norm_mix[0][None, :], wq, tm, 512)
    pad = CA_LEFT_CHUNKS * CHUNK
    qi = jnp.arange(BAND_SUB)[:, None]
    kj = jnp.arange(BAND_WIN)[None, :]
    in_band = jnp.logical_and(kj // CHUNK >= qi // CHUNK, kj // CHUNK <= qi // CHUNK + CA_LEFT_CHUNKS)
    m_len = BAND_WIN + BAND_SUB + 1
    rel = jnp.arange(m_len) - BAND_SUB - pad
    e = odd_rel_bias[0].astype(F32)[:, jnp.clip(rel, -REL_CLIP, REL_CLIP) + REL_CLIP]
    bias = jnp.tile(e, (1, BAND_SUB))[:, :BAND_SUB * (m_len - 1)].reshape(CA_HEADS, BAND_SUB, m_len - 1)
    bias = jnp.where(in_band[None], bias[:, :, BAND_SUB:BAND_SUB + BAND_WIN] * log2e, MASKED)
    o_c = _band_attn(qkv, bias, seq, tm)
    xr = mix_ffn([o_c], [odd_w_out[0].astype(BF16)], xr, 1, True)
    return xr.reshape(batch, seq, d)
```

```python
import functools
import math

import jax
import jax.numpy as jnp
from jax import lax
from jax.experimental import pallas as pl
from jax.experimental.pallas import tpu as pltpu

F32 = jnp.float32
BF16 = jnp.bfloat16

D_MODEL = 1024
CHUNK = 64
RMS_EPS = 1e-6
ROPE_THETA = 10000.0
DN_HEADS = 4
DN_DK = 128
DN_QK = 512
DN_CONV = 4
DIFF_HEADS = 4
DIFF_HD = 64
CA_HEADS = 16
CA_HD = 64
CA_LEFT_CHUNKS = 8
REL_CLIP = 256
D_FF = 2816
MASKED = -1e30
VMEM_LIMIT = 56 * 1024 * 1024
LANES = 128
BAND_SUB = 128
BAND_WIN = BAND_SUB + CA_LEFT_CHUNKS * CHUNK

def _params(*sem):
    return pltpu.CompilerParams(dimension_semantics=sem, vmem_limit_bytes=VMEM_LIMIT)


def _rms(x, g):
    ms = jnp.mean(x * x, axis=-1, keepdims=True)
    return x * lax.rsqrt(ms + RMS_EPS) * g


def _silu(x):
    return x * jax.nn.sigmoid(x)


def _dot(a, b):
    return jnp.dot(a, b, preferred_element_type=F32)


def _dot_nt(a, b):
    return lax.dot_general(a, b, (((1,), (1,)), ((), ())), preferred_element_type=F32)


def _dot_tn(a, b):
    return lax.dot_general(a, b, (((0,), (0,)), ((), ())), preferred_element_type=F32)


def _dot_exact_lhs(l_bf16, x):
    x1 = x.astype(BF16)
    x2 = (x - x1.astype(F32)).astype(BF16)
    return _dot(l_bf16, x1) + _dot(l_bf16, x2)


def _resident(shape):
    return pl.BlockSpec(shape, lambda *_: (0,) * len(shape), pipeline_mode=pl.Buffered(1))


def _norm_matmul_kernel(x_ref, g_ref, w_ref, o_ref, *, tn):
    xn = _rms(x_ref[...], g_ref[...]).astype(BF16)
    for n0 in range(0, w_ref.shape[1], tn):
        o_ref[:, n0:n0 + tn] = _dot(xn, w_ref[:, n0:n0 + tn]).astype(o_ref.dtype)


def _norm_matmul(x, g, w, tm, tn):
    t, d = x.shape
    n = w.shape[1]
    return pl.pallas_call(
        functools.partial(_norm_matmul_kernel, tn=tn),
        grid=(t // tm,),
        in_specs=[pl.BlockSpec((tm, d), lambda i: (i, 0)), _resident((1, d)), _resident(w.shape)],
        out_specs=pl.BlockSpec((tm, n), lambda i: (i, 0)),
        out_shape=jax.ShapeDtypeStruct((t, n), BF16),
        compiler_params=_params("parallel"),
        name="norm_matmul",
    )(x, g, w)


def _in_proj0_kernel(x_ref, g_ref, w_ref, whi_ref, wlo_ref, cos_ref, sin_ref,
                     dn_ref, ab_ref, qa_ref, qb_ref, kr_ref, vt_ref, *, tn):
    xn = _rms(x_ref[...], g_ref[...])
    hi = xn.astype(BF16)
    lo_x = (xn - hi.astype(F32)).astype(BF16)
    ab_ref[...] = _dot(hi, whi_ref[...]) + (_dot(hi, wlo_ref[...]) + _dot(lo_x, whi_ref[...]))
    n_dn = dn_ref.shape[1]
    for n0 in range(0, n_dn, tn):
        dn_ref[:, n0:n0 + tn] = _dot(hi, w_ref[:, n0:n0 + tn]).astype(BF16)

    w = DIFF_HEADS * LANES
    fq = _dot(hi, w_ref[:, n_dn:n_dn + w])
    fk = _dot(hi, w_ref[:, n_dn + w:n_dn + 2 * w])
    fv = _dot(hi, w_ref[:, n_dn + 2 * w:n_dn + 3 * w])
    cos = cos_ref[...]
    sin = sin_ref[...]
    lane = lax.broadcasted_iota(jnp.int32, cos.shape, 1)
    first = (lane % DIFF_HD) < (DIFF_HD // 2)
    lo = lane < DIFF_HD

    def rope(x):
        rot = jnp.where(first, pltpu.roll(x, LANES - DIFF_HD // 2, 1), pltpu.roll(x, DIFF_HD // 2, 1))
        return x * cos + rot * sin

    for h in range(DIFF_HEADS):
        hs = slice(h * LANES, (h + 1) * LANES)
        q = rope(fq[:, hs]) * (DIFF_HD ** -0.5 * math.log2(math.e))
        qa_ref[:, hs] = jnp.where(lo, q, 0.0).astype(BF16)
        qb_ref[:, hs] = jnp.where(lo, 0.0, q).astype(BF16)
        kr_ref[:, hs] = rope(fk[:, hs]).astype(BF16)
        vt_ref[h] = fv[:, hs].T.astype(BF16)


def _in_proj0(x, g, w, w_hp, cos_t, sin_t, batch, seq, tm, tn):
    t, d = x.shape
    w_diff = DIFF_HEADS * LANES
    n_dn = w.shape[1] - 3 * w_diff
    spt = seq // tm
    row = lambda n: pl.BlockSpec((tm, n), lambda i: (i, 0))
    tab = pl.BlockSpec((tm, LANES), lambda i: (i % spt, 0))
    qkr = jax.ShapeDtypeStruct((t, w_diff), BF16)
    return pl.pallas_call(
        functools.partial(_in_proj0_kernel, tn=tn),
        grid=(t // tm,),
        in_specs=[row(d), _resident((1, d)), _resident(w.shape), _resident(w_hp[0].shape),
                  _resident(w_hp[1].shape), tab, tab],
        out_specs=[row(n_dn), row(w_hp[0].shape[1]), row(w_diff), row(w_diff), row(w_diff),
                   pl.BlockSpec((None, DIFF_HEADS, None, LANES, tm), lambda i: (i // spt, 0, i % spt, 0, 0))],
        out_shape=[jax.ShapeDtypeStruct((t, n_dn), BF16), jax.ShapeDtypeStruct((t, w_hp[0].shape[1]), F32),
                   qkr, qkr, qkr, jax.ShapeDtypeStruct((batch, DIFF_HEADS, spt, LANES, tm), BF16)],
        compiler_params=_params("parallel"),
        name="in_proj0",
    )(x, g, w, w_hp[0], w_hp[1], cos_t, sin_t)


def _mix_ffn_kernel(*refs, trans, tm, tf, tiles_per_batch, final_norm):
    n_in = len(trans)
    a_refs = refs[:n_in]
    w_refs = refs[n_in:2 * n_in]
    x_ref, g_ref, wi_ref, cw_ref, wo_ref, fg_ref, o_ref, gs_ref, act_ref = refs[2 * n_in:]
    i = pl.program_id(0)

    @pl.when(i % tiles_per_batch == 0)
    def _():
        gs_ref[0:8, :] = jnp.zeros((8, D_FF), F32)

    x = x_ref[...]
    for a_ref, w_ref, tr in zip(a_refs, w_refs, trans):
        x = x + (_dot_tn if tr else _dot)(a_ref[...], w_ref[...])
    xn = _rms(x, g_ref[...]).astype(BF16)
    for c0 in range(0, D_FF, tf):
        cs = slice(c0, c0 + tf)
        gate = _dot(xn, wi_ref[:, cs])
        up = _dot(xn, wi_ref[:, D_FF + c0:D_FF + c0 + tf])
        gs_ref[8:, cs] = gate
        cw = cw_ref[:, cs]
        y = (cw[2:3] * gate + cw[1:2] * gs_ref[pl.ds(7, tm), cs]
             + cw[0:1] * gs_ref[pl.ds(6, tm), cs] + cw[3:4])
        act_ref[:, cs] = (_silu(y) * up).astype(BF16)
    gs_ref[0:8, :] = gs_ref[tm:tm + 8, :]
    r = x + _dot(act_ref[...], wo_ref[...])
    o_ref[...] = _rms(r, fg_ref[...]) if final_norm else r


def _mix_ffn(acts, ws, x, g, w_in, cwb, w_out, final_g, seq, tm, tf, final_norm):
    t, d = x.shape
    tpb = seq // tm
    trans = tuple(a.ndim == 3 for a in acts)
    kern = functools.partial(_mix_ffn_kernel, trans=trans, tm=tm, tf=tf, tiles_per_batch=tpb,
                             final_norm=final_norm)
    in_specs = ([pl.BlockSpec((None, a.shape[1], tm), lambda i: (i // tpb, 0, i % tpb)) if tr
                 else pl.BlockSpec((tm, a.shape[1]), lambda i: (i, 0)) for a, tr in zip(acts, trans)]
                + [_resident(w.shape) for w in ws]
                + [pl.BlockSpec((tm, d), lambda i: (i, 0)),
                   _resident((1, d)),
                   _resident(w_in.shape),
                   _resident(cwb.shape),
                   _resident(w_out.shape),
                   _resident((1, d))])
    return pl.pallas_call(
        kern,
        grid=(t // tm,),
        in_specs=in_specs,
        out_specs=pl.BlockSpec((tm, d), lambda i: (i, 0)),
        out_shape=jax.ShapeDtypeStruct((t, d), F32),
        scratch_shapes=[pltpu.VMEM((tm + 8, D_FF), F32),
                        pltpu.VMEM((tm, D_FF), BF16)],
        compiler_params=_params("arbitrary"),
        name="mix_ffn",
    )(*acts, *ws, x, g, w_in, cwb, w_out, final_g)


def _diff_attn_kernel(qa_ref, qb_ref, k_ref, vt_ref, lam_ref, gain_ref, o_ref, acc_ref, s_ref, *, tq, tk):
    qi = pl.program_id(2)
    acc_ref[...] = jnp.zeros(acc_ref.shape, F32)
    qs = (qa_ref[...], qb_ref[...])
    assert tq == 2 * tk

    def scores(ki, buf, q0=0):
        off = pl.multiple_of(ki * tk, tk)
        kt = k_ref[pl.ds(off, tk), :]
        for mp in range(2):
            s_ref[buf, mp, :, q0:] = _dot_nt(kt, qs[mp][q0:])

    def softmax_pv(ki, buf, carry, mask, q0=0):
        vt = vt_ref[ki]
        new, ps, alphas = [], [], []
        for mp in range(2):
            m_old, l_old = carry[2 * mp], carry[2 * mp + 1]
            s = s_ref[buf, mp, :, q0:]
            if mask is not None:
                s = jnp.where(mask[:, q0:], s, MASKED)
            m_new = jnp.maximum(m_old[:, q0:], jnp.max(s, axis=0, keepdims=True))
            alpha = jnp.exp2(m_old[:, q0:] - m_new)
            p = jnp.exp2(s - m_new)
            l_new = alpha * l_old[:, q0:] + jnp.sum(p, axis=0, keepdims=True)
            ps.append(p.astype(BF16))
            alphas.append(alpha)
            if q0:
                m_new = jnp.concatenate([m_old[:, :q0], m_new], axis=1)
                l_new = jnp.concatenate([l_old[:, :q0], l_new], axis=1)
            new += [m_new, l_new]
        for mp in range(2):
            acc_ref[mp, :, q0:] = alphas[mp] * acc_ref[mp, :, q0:] + _dot(vt, ps[mp])
        return tuple(new)

    row0 = jnp.full((1, tq), MASKED, F32)
    zero = jnp.zeros((1, tq), F32)

    scores(0, 0)

    def body(kj, c):
        scores(2 * kj + 1, 1)
        c = softmax_pv(2 * kj, 0, c, None)
        scores(2 * kj + 2, 0)
        return softmax_pv(2 * kj + 1, 1, c, None)

    carry = lax.fori_loop(0, qi, body, (row0, zero, row0, zero))

    key = lax.broadcasted_iota(jnp.int32, (tk, tq), 0)
    qry = lax.broadcasted_iota(jnp.int32, (tk, tq), 1)
    scores(2 * qi + 1, 1, tk)
    carry = softmax_pv(2 * qi, 0, carry, (key // CHUNK) <= (qry // CHUNK))
    carry = softmax_pv(2 * qi + 1, 1, carry, (key // CHUNK + tk // CHUNK) <= (qry // CHUNK), tk)
    _, l1, _, l2 = carry

    o = acc_ref[0] / l1 - lam_ref[...] * (acc_ref[1] / l2)
    ms = jnp.mean(o * o, axis=0, keepdims=True)
    o_ref[...] = (o * lax.rsqrt(ms + RMS_EPS) * gain_ref[...]).astype(o_ref.dtype)


def _diff_attn(qa, qb, kr, vt, lam_row, gain_t, batch, seq, tq):
    nq = seq // tq
    nk, _, tk = vt.shape[2:]
    kern = functools.partial(_diff_attn_kernel, tq=tq, tk=tk)
    qspec = pl.BlockSpec((tq, LANES), lambda b, h, q: (b * nq + q, h))
    return pl.pallas_call(
        kern,
        grid=(batch, DIFF_HEADS, nq),
        in_specs=[qspec, qspec,
                  pl.BlockSpec((seq, LANES), lambda b, h, q: (b, h)),
                  pl.BlockSpec((None, None, nk, LANES, tk), lambda b, h, q: (b, h, 0, 0, 0)),
                  pl.BlockSpec((1, tq), lambda b, h, q: (0, 0)),
                  pl.BlockSpec((LANES, tq), lambda b, h, q: (0, 0))],
        out_specs=pl.BlockSpec((None, LANES, tq), lambda b, h, q: (b, h, q)),
        out_shape=jax.ShapeDtypeStruct((batch, DIFF_HEADS * LANES, seq), BF16),
        scratch_shapes=[pltpu.VMEM((2, LANES, tq), F32),
                        pltpu.VMEM((2, 2, tk, tq), F32)],
        compiler_params=_params("parallel", "parallel", "arbitrary"),
        name="diff_attn",
    )(qa, qb, kr, vt, lam_row, gain_t)


def _band_attn_kernel(q_ref, kc_ref, kp_ref, vc_ref, vp_ref, bias_ref, o_ref,
                      kw_ref, vw_ref, *, tq, tiles_per_batch):
    i = pl.program_id(0)
    pad = CA_LEFT_CHUNKS * CHUNK
    kw_ref[0:tq, :] = kp_ref[...]
    kw_ref[tq:, :] = kc_ref[...]
    vw_ref[0:tq, :] = vp_ref[...]
    vw_ref[tq:, :] = vc_ref[...]
    has_prev = (i % tiles_per_batch) != 0
    lane = lax.broadcasted_iota(jnp.int32, (BAND_SUB, LANES), 1)
    lo = lane < CA_HD
    col = lax.broadcasted_iota(jnp.int32, (BAND_SUB, BAND_WIN), 1)

    def sub_tile(s, carry, first):
        r0 = pl.multiple_of(s * BAND_SUB, BAND_SUB)
        w0 = pl.multiple_of(r0 + (tq - pad), BAND_SUB)
        css = [slice(pr * LANES, (pr + 1) * LANES) for pr in range(CA_HEADS // 2)]
        scs = []
        for pr, cs in enumerate(css):
            qp = q_ref[pl.ds(r0, BAND_SUB), cs]
            kw = kw_ref[pl.ds(w0, BAND_WIN), cs]
            for hh in range(2):
                qh = jnp.where(lo, qp, 0) if hh == 0 else jnp.where(lo, 0, qp)
                scs.append(_dot_nt(qh.astype(BF16), kw))
        ps, dens = [], []
        for h, sc in enumerate(scs):
            sc = sc + bias_ref[h]
            if first:
                sc = jnp.where(col + w0 >= tq, sc, MASKED)
            p = jnp.exp2(sc - jnp.max(sc, axis=-1, keepdims=True))
            dens.append(jnp.sum(p, axis=-1, keepdims=True))
            ps.append(p.astype(BF16))
        for pr, cs in enumerate(css):
            vw = vw_ref[pl.ds(w0, BAND_WIN), cs]
            outs = [_dot(ps[2 * pr + hh], vw) / dens[2 * pr + hh] for hh in range(2)]
            o_ref[pl.ds(r0, BAND_SUB), cs] = jnp.where(lo, outs[0], outs[1]).astype(o_ref.dtype)
        return carry

    @pl.when(has_prev)
    def _():
        lax.fori_loop(0, tq // BAND_SUB, lambda s, c: sub_tile(s, c, False), 0)

    @pl.when(jnp.logical_not(has_prev))
    def _():
        lax.fori_loop(0, tq // BAND_SUB, lambda s, c: sub_tile(s, c, True), 0)


def _band_attn(qkv, bias, seq, tq):
    t = qkv.shape[0]
    d = CA_HEADS * CA_HD
    tpb = seq // tq
    kern = functools.partial(_band_attn_kernel, tq=tq, tiles_per_batch=tpb)

    def prev(i):
        return jnp.where(i % tpb == 0, i, i - 1)

    return pl.pallas_call(
        kern,
        grid=(t // tq,),
        in_specs=[pl.BlockSpec((tq, d), lambda i: (i, 0)),
                  pl.BlockSpec((tq, d), lambda i: (i, 1)),
                  pl.BlockSpec((tq, d), lambda i: (prev(i), 1)),
                  pl.BlockSpec((tq, d), lambda i: (i, 2)),
                  pl.BlockSpec((tq, d), lambda i: (prev(i), 2)),
                  pl.BlockSpec(bias.shape, lambda i: (0, 0, 0))],
        out_specs=pl.BlockSpec((tq, d), lambda i: (i, 0)),
        out_shape=jax.ShapeDtypeStruct((t, d), BF16),
        scratch_shapes=[pltpu.VMEM((2 * tq, d), BF16),
                        pltpu.VMEM((2 * tq, d), BF16)],
        compiler_params=_params("parallel"),
        name="band_attn",
    )(qkv, qkv, qkv, qkv, qkv, bias)


def _deltanet_kernel(q_ref, k_ref, v_ref, qp_ref, kp_ref, vp_ref, ab_ref, gate_ref,
                     cw_ref, ea_ref, dtb_ref, ng_ref, o_ref, state_ref, xx_ref, *, tm):
    i = pl.program_id(1)
    pair = 2 * CHUNK

    @pl.when(i == 0)
    def _():
        state_ref[...] = jnp.zeros(state_ref.shape, F32)

    keep = (i > 0).astype(F32)
    hist = 16

    def conv_silu(cur_ref, prev_ref, c0):
        xx_ref[0:hist, :] = prev_ref[...].astype(F32) * keep
        cur = cur_ref[...].astype(F32)
        xx_ref[hist:, :] = cur
        w = cw_ref[:, c0:c0 + DN_QK]
        y = w[DN_CONV - 1:DN_CONV] * cur
        for s in range(1, DN_CONV):
            y = y + w[DN_CONV - 1 - s:DN_CONV - s] * xx_ref[pl.ds(hist - s, tm), :]
        return _silu(y)

    qc = conv_silu(q_ref, qp_ref, 0)
    kc = conv_silu(k_ref, kp_ref, DN_QK)
    vc = conv_silu(v_ref, vp_ref, 2 * DN_QK)

    ab = ab_ref[...]
    g_all = -ea_ref[...] * jax.nn.softplus(ab + dtb_ref[...])
    beta_all = jax.nn.sigmoid(ab)

    row = lax.broadcasted_iota(jnp.int32, (pair, pair), 0)
    col = lax.broadcasted_iota(jnp.int32, (pair, pair), 1)
    same = (row // CHUNK) == (col // CHUNK)
    tri = jnp.logical_and(same, col <= row)
    strict = jnp.logical_and(same, col < row)
    lmat = jnp.where(tri, 1.0, 0.0).astype(BF16)
    su = jnp.where(jnp.logical_and(same, row > col), 1.0, 0.0)
    eye = jnp.where(row == col, 1.0, 0.0)
    top = row < CHUNK

    npair = tm // pair
    units = [(h, dc) for h in range(DN_HEADS) for dc in range(npair)]
    nu = len(units)
    hsl = [slice(h * DN_DK, (h + 1) * DN_DK) for h in range(DN_HEADS)]
    rsl = [slice(dc * pair, (dc + 1) * pair) for dc in range(npair)]

    def bf(x):
        return x.astype(BF16)

    qn, kn, g_b, beta_b = [], [], [], []
    for h in range(DN_HEADS):
        qh = qc[:, hsl[h]]
        kh = kc[:, hsl[h]]
        qn.append(qh * lax.rsqrt(jnp.sum(qh * qh, axis=-1, keepdims=True) + RMS_EPS) * (DN_DK ** -0.5))
        kn.append(kh * lax.rsqrt(jnp.sum(kh * kh, axis=-1, keepdims=True) + RMS_EPS))
        g_b.append(jnp.broadcast_to(g_all[:, h:h + 1], (tm, pair)))
        beta_b.append(jnp.broadcast_to(beta_all[:, DN_HEADS + h:DN_HEADS + h + 1], (tm, pair)))
    q2 = [qn[h][rsl[dc]] for h, dc in units]
    k2 = [kn[h][rsl[dc]] for h, dc in units]
    v2 = [vc[rsl[dc], hsl[h]] for h, dc in units]
    b2 = [beta_b[h][rsl[dc]] for h, dc in units]
    k2b = [bf(x) for x in k2]

    g_cat = jnp.concatenate([g_b[h][rsl[dc]] for h, dc in units], axis=1)
    gcum_cat = _dot_exact_lhs(lmat, g_cat)
    gdiff_cat = _dot_exact_lhs(lmat, g_cat * jnp.concatenate([su] * nu, axis=1))
    usl = [slice(u * pair, (u + 1) * pair) for u in range(nu)]
    gcum = [gcum_cat[:, s] for s in usl]
    decay = [jnp.where(tri, jnp.exp(jnp.where(tri, gdiff_cat[:, s], 0.0)), 0.0) for s in usl]
    kk = [_dot_nt(x, x) for x in k2b]
    a = [jnp.where(strict, b2[u] * kk[u] * decay[u], 0.0) for u in range(nu)]
    tinv = [eye - x for x in a]
    pw = a
    for _ in range(5):
        pwb = [bf(x) for x in pw]
        pw = [_dot(x, x) for x in pwb]
        pwb = [bf(x) for x in pw]
        tinv = [tinv[u] + _dot(bf(tinv[u]), pwb[u]) for u in range(nu)]
    eg = [jnp.exp(x) for x in gcum]
    rhs = [jnp.concatenate([v2[u] * b2[u], k2[u] * b2[u] * eg[u]], axis=1) for u in range(nu)]
    uw = [_dot(bf(tinv[u]), bf(rhs[u])) for u in range(nu)]
    qk = [jnp.where(tri, _dot_nt(bf(q2[u]), k2b[u]) * decay[u], 0.0) for u in range(nu)]
    qd = [bf(q2[u] * eg[u]) for u in range(nu)]
    gl = [(gcum[u][CHUNK - 1:CHUNK], gcum[u][pair - 1:pair]) for u in range(nu)]
    kd = [bf(k2[u] * jnp.exp(jnp.where(top, gl[u][0], gl[u][1]) - gcum[u])) for u in range(nu)]
    egl = [(jnp.exp(gl[u][0]), jnp.exp(gl[u][1])) for u in range(nu)]

    st = [state_ref[h] for h in range(DN_HEADS)]
    vns = [[None, None] for _ in range(nu)]
    ois = [[None, None] for _ in range(nu)]
    for dc in range(npair):
        for c in range(2):
            cr = slice(c * CHUNK, (c + 1) * CHUNK)
            us = [h * npair + dc for h in range(DN_HEADS)]
            stb = [bf(x) for x in st]
            vn = [uw[u][cr, :DN_DK] - _dot(bf(uw[u][cr, DN_DK:]), stb[h]) for h, u in enumerate(us)]
            for h, u in enumerate(us):
                ois[u][c] = _dot(qd[u][cr], stb[h])
                vns[u][c] = vn[h]
            st = [st[h] * egl[u][c] + _dot_tn(kd[u][cr], bf(vn[h])) for h, u in enumerate(us)]
    for h in range(DN_HEADS):
        state_ref[h] = st[h]

    for u, (h, dc) in enumerate(units):
        vn2 = jnp.concatenate(vns[u], axis=0)
        o = jnp.concatenate(ois[u], axis=0) + _dot(bf(qk[u]), bf(vn2))
        o = _rms(o, ng_ref[...]) * _silu(gate_ref[rsl[dc], hsl[h]].astype(F32))
        o_ref[rsl[dc], hsl[h]] = o.astype(o_ref.dtype)


def _deltanet(p0, ab, cwp, ea, dtb, ng, batch, seq, tm):
    t = p0.shape[0]
    w = DN_QK
    nt = seq // tm
    hist = 16
    kern = functools.partial(_deltanet_kernel, tm=tm)

    def cur(cb):
        return pl.BlockSpec((tm, w), lambda b, i: (b * nt + i, cb))

    def prev(cb):
        return pl.BlockSpec((hist, w), lambda b, i: (jnp.maximum((b * nt + i) * (tm // hist) - 1, 0), cb))

    vec = pl.BlockSpec((1, LANES), lambda b, i: (0, 0))
    return pl.pallas_call(
        kern,
        grid=(batch, nt),
        in_specs=[cur(0), cur(1), cur(2), prev(0), prev(1), prev(2),
                  pl.BlockSpec((tm, LANES), lambda b, i: (b * nt + i, 0)),
                  cur(3),
                  pl.BlockSpec(cwp.shape, lambda b, i: (0, 0)),
                  vec, vec, vec],
        out_specs=pl.BlockSpec((tm, w), lambda b, i: (b * nt + i, 0)),
        out_shape=jax.ShapeDtypeStruct((t, w), BF16),
        scratch_shapes=[pltpu.VMEM((DN_HEADS, DN_DK, DN_DK), F32),
                        pltpu.VMEM((tm + hist, w), F32)],
        compiler_params=_params("arbitrary", "arbitrary"),
        name="deltanet",
    )(p0, p0, p0, p0, p0, p0, ab, p0, cwp, ea, dtb, ng)


def _tile(n, want):
    return min(n, want)


def _pad_rows(a, rows):
    return jnp.concatenate([a, jnp.zeros((rows - a.shape[0], a.shape[1]), a.dtype)], axis=0)


def kernel(x, even_norm_mix, even_w_in, even_dn_conv, even_dn_A_log, even_dn_dt_bias, even_dn_norm,
           even_diff_lambda, even_diff_subln, even_w_out, odd_norm_mix, odd_w_qkv, odd_rel_bias,
           odd_w_out, ffn_norm, ffn_w_in, ffn_conv_w, ffn_conv_b, ffn_w_out, final_norm):
    batch, seq, d = x.shape
    t = batch * seq
    xr = x.reshape(t, d)
    tm = _tile(seq, 512)

    def mix_ffn(acts, ws, xr, i, final):
        cwb = _pad_rows(jnp.concatenate([ffn_conv_w[i], ffn_conv_b[i][None, :]], axis=0), 8)
        return _mix_ffn(acts, ws, xr, ffn_norm[i][None, :], ffn_w_in[i].astype(BF16), cwb,
                        ffn_w_out[i].astype(BF16), final_norm[None, :], seq, tm, 256, final)

    w0 = even_w_in[0]
    n_qkv = 3 * DN_QK
    w_main = jnp.concatenate([w0[:, :n_qkv], w0[:, n_qkv + 2 * DN_HEADS:]], axis=1).astype(BF16)
    w_ab = jnp.pad(w0[:, n_qkv:n_qkv + 2 * DN_HEADS], ((0, 0), (0, LANES - 2 * DN_HEADS)))
    w_ab_hi = w_ab.astype(BF16)
    w_ab_lo = (w_ab - w_ab_hi.astype(F32)).astype(BF16)
    g0 = even_norm_mix[0][None, :]
    half = DIFF_HD // 2
    inv = 1.0 / (ROPE_THETA ** (jnp.arange(0, DIFF_HD, 2, dtype=F32) / DIFF_HD))
    ang = jnp.arange(seq, dtype=F32)[:, None] * inv[None, :]
    cos_t = jnp.tile(jnp.cos(ang), (1, LANES // half))
    sign = jnp.tile(jnp.concatenate([-jnp.ones((half,), F32), jnp.ones((half,), F32)]), LANES // DIFF_HD)
    sin_t = jnp.tile(jnp.sin(ang), (1, LANES // half)) * sign[None, :]
    p0, ab, qa, qb, kr, vt = _in_proj0(xr, g0, w_main, (w_ab_hi, w_ab_lo), cos_t, sin_t, batch, seq, tm, 512)

    def lane_vec(v):
        return jnp.pad(v.astype(F32), (0, LANES - v.shape[0]))[None, :]

    o_a = _deltanet(p0, ab, _pad_rows(even_dn_conv[0], 8), lane_vec(jnp.exp(even_dn_A_log[0])),
                    lane_vec(even_dn_dt_bias[0]), even_dn_norm[0][None, :], batch, seq, _tile(seq, 512))

    lam_init = 0.8 - 0.6 * math.exp(-0.3 * 0)
    lp = even_diff_lambda[0].astype(F32)
    lam = jnp.exp(jnp.sum(lp[0] * lp[1])) - jnp.exp(jnp.sum(lp[2] * lp[3])) + lam_init
    tq = _tile(seq, 1024)
    lam_row = jnp.full((1, tq), lam, F32)
    gain_t = jnp.broadcast_to((even_diff_subln[0] * (1.0 - lam_init))[:, None], (LANES, tq))
    o_b = _diff_attn(qa, qb, kr, vt, lam_row, gain_t, batch, seq, tq)

    wo = even_w_out[0].astype(BF16)
    xr = mix_ffn([o_a, o_b], [wo[:DN_QK], wo[DN_QK:]], xr, 0, False)

    wq = odd_w_qkv[0]
    log2e = math.log2(math.e)
    wq = jnp.concatenate([wq[:, :D_MODEL] * (CA_HD ** -0.5 * log2e), wq[:, D_MODEL:]], axis=1).astype(BF16)
    qkv = _norm_matmul(xr, odd_norm_mix[0][None, :], wq, tm, 512)
    pad = CA_LEFT_CHUNKS * CHUNK
    qi = jnp.arange(BAND_SUB)[:, None]
    kj = jnp.arange(BAND_WIN)[None, :]
    in_band = jnp.logical_and(kj // CHUNK >= qi // CHUNK, kj // CHUNK <= qi // CHUNK + CA_LEFT_CHUNKS)
    m_len = BAND_WIN + BAND_SUB + 1
    rel = jnp.arange(m_len) - BAND_SUB - pad
    e = odd_rel_bias[0].astype(F32)[:, jnp.clip(rel, -REL_CLIP, REL_CLIP) + REL_CLIP]
    bias = jnp.tile(e, (1, BAND_SUB))[:, :BAND_SUB * (m_len - 1)].reshape(CA_HEADS, BAND_SUB, m_len - 1)
    bias = jnp.where(in_band[None], bias[:, :, BAND_SUB:BAND_SUB + BAND_WIN] * log2e, MASKED)
    o_c = _band_attn(qkv, bias, seq, tm)
    xr = mix_ffn([o_c], [odd_w_out[0].astype(BF16)], xr, 1, True)
    return xr.reshape(batch, seq, d)
```

```python
import functools
import math

import jax
import jax.numpy as jnp
from jax import lax
from jax.experimental import pallas as pl
from jax.experimental.pallas import tpu as pltpu

F32 = jnp.float32
BF16 = jnp.bfloat16

D_MODEL = 1024
CHUNK = 64
RMS_EPS = 1e-6
ROPE_THETA = 10000.0
DN_HEADS = 4
DN_DK = 128
DN_QK = 512
DN_CONV = 4
DIFF_HEADS = 4
DIFF_HD = 64
CA_HEADS = 16
CA_HD = 64
CA_LEFT_CHUNKS = 8
REL_CLIP = 256
D_FF = 2816
MASKED = -1e30
VMEM_LIMIT = 56 * 1024 * 1024
LANES = 128
BAND_SUB = 128
BAND_WIN = BAND_SUB + CA_LEFT_CHUNKS * CHUNK

def _params(*sem):
    return pltpu.CompilerParams(dimension_semantics=sem, vmem_limit_bytes=VMEM_LIMIT)


def _rms(x, g):
    ms = jnp.mean(x * x, axis=-1, keepdims=True)
    return x * lax.rsqrt(ms + RMS_EPS) * g


def _silu(x):
    return x * jax.nn.sigmoid(x)


def _dot(a, b):
    return jnp.dot(a, b, preferred_element_type=F32)


def _dot_nt(a, b):
    return lax.dot_general(a, b, (((1,), (1,)), ((), ())), preferred_element_type=F32)


def _dot_tn(a, b):
    return lax.dot_general(a, b, (((0,), (0,)), ((), ())), preferred_element_type=F32)


def _dot_exact_lhs(l_bf16, x):
    x1 = x.astype(BF16)
    x2 = (x - x1.astype(F32)).astype(BF16)
    return _dot(l_bf16, x1) + _dot(l_bf16, x2)


def _resident(shape):
    return pl.BlockSpec(shape, lambda *_: (0,) * len(shape), pipeline_mode=pl.Buffered(1))


def _norm_matmul_kernel(x_ref, g_ref, w_ref, o_ref, *, tn):
    xn = _rms(x_ref[...], g_ref[...]).astype(BF16)
    for n0 in range(0, w_ref.shape[1], tn):
        o_ref[:, n0:n0 + tn] = _dot(xn, w_ref[:, n0:n0 + tn]).astype(o_ref.dtype)


def _norm_matmul(x, g, w, tm, tn):
    t, d = x.shape
    n = w.shape[1]
    return pl.pallas_call(
        functools.partial(_norm_matmul_kernel, tn=tn),
        grid=(t // tm,),
        in_specs=[pl.BlockSpec((tm, d), lambda i: (i, 0)), _resident((1, d)), _resident(w.shape)],
        out_specs=pl.BlockSpec((tm, n), lambda i: (i, 0)),
        out_shape=jax.ShapeDtypeStruct((t, n), BF16),
        compiler_params=_params("parallel"),
        name="norm_matmul",
    )(x, g, w)


def _in_proj0_kernel(x_ref, g_ref, w_ref, whi_ref, wlo_ref, cw_ref, cos_ref, sin_ref,
                     dn_ref, ab_ref, qa_ref, qb_ref, kr_ref, vt_ref, xx_ref, *, tm, tn, tiles_per_batch):
    i = pl.program_id(0)
    n_conv = cw_ref.shape[1]

    @pl.when(i % tiles_per_batch == 0)
    def _():
        xx_ref[0:8, :] = jnp.zeros((8, n_conv), F32)

    xn = _rms(x_ref[...], g_ref[...])
    hi = xn.astype(BF16)
    lo_x = (xn - hi.astype(F32)).astype(BF16)
    ab_ref[...] = _dot(hi, whi_ref[...]) + (_dot(hi, wlo_ref[...]) + _dot(lo_x, whi_ref[...]))
    n_dn = dn_ref.shape[1]
    for n0 in range(0, n_dn, tn):
        cs = slice(n0, n0 + tn)
        if n0 < n_conv:
            xx_ref[8:, cs] = _dot(hi, w_ref[:, cs])
        else:
            dn_ref[:, cs] = _dot(hi, w_ref[:, cs]).astype(BF16)
    w = DIFF_HEADS * LANES
    fq = _dot(hi, w_ref[:, n_dn:n_dn + w])
    fk = _dot(hi, w_ref[:, n_dn + w:n_dn + 2 * w])
    fv = _dot(hi, w_ref[:, n_dn + 2 * w:n_dn + 3 * w])

    for n0 in range(0, n_conv, tn):
        cs = slice(n0, n0 + tn)
        cw = cw_ref[:, cs]
        y = cw[DN_CONV - 1:DN_CONV] * xx_ref[pl.ds(8, tm), cs]
        for s in range(1, DN_CONV):
            y = y + cw[DN_CONV - 1 - s:DN_CONV - s] * xx_ref[pl.ds(8 - s, tm), cs]
        dn_ref[:, cs] = _silu(y).astype(BF16)
    xx_ref[0:8, :] = xx_ref[tm:tm + 8, :]

    cos = cos_ref[...]
    sin = sin_ref[...]
    lane = lax.broadcasted_iota(jnp.int32, cos.shape, 1)
    first = (lane % DIFF_HD) < (DIFF_HD // 2)
    lo = lane < DIFF_HD

    def rope(x):
        rot = jnp.where(first, pltpu.roll(x, LANES - DIFF_HD // 2, 1), pltpu.roll(x, DIFF_HD // 2, 1))
        return x * cos + rot * sin

    for h in range(DIFF_HEADS):
        hs = slice(h * LANES, (h + 1) * LANES)
        q = rope(fq[:, hs]) * (DIFF_HD ** -0.5 * math.log2(math.e))
        qa_ref[:, hs] = jnp.where(lo, q, 0.0).astype(BF16)
        qb_ref[:, hs] = jnp.where(lo, 0.0, q).astype(BF16)
        kr_ref[:, hs] = rope(fk[:, hs]).astype(BF16)
        vt_ref[h] = fv[:, hs].T.astype(BF16)


def _in_proj0(x, g, w, w_hp, cwp, cos_t, sin_t, batch, seq, tm, tn):
    t, d = x.shape
    w_diff = DIFF_HEADS * LANES
    n_dn = w.shape[1] - 3 * w_diff
    spt = seq // tm
    row = lambda n: pl.BlockSpec((tm, n), lambda i: (i, 0))
    tab = pl.BlockSpec((tm, LANES), lambda i: (i % spt, 0))
    qkr = jax.ShapeDtypeStruct((t, w_diff), BF16)
    return pl.pallas_call(
        functools.partial(_in_proj0_kernel, tm=tm, tn=tn, tiles_per_batch=spt),
        grid=(t // tm,),
        in_specs=[row(d), _resident((1, d)), _resident(w.shape), _resident(w_hp[0].shape),
                  _resident(w_hp[1].shape), _resident(cwp.shape), tab, tab],
        out_specs=[row(n_dn), row(w_hp[0].shape[1]), row(w_diff), row(w_diff), row(w_diff),
                   pl.BlockSpec((None, DIFF_HEADS, None, LANES, tm), lambda i: (i // spt, 0, i % spt, 0, 0))],
        out_shape=[jax.ShapeDtypeStruct((t, n_dn), BF16), jax.ShapeDtypeStruct((t, w_hp[0].shape[1]), F32),
                   qkr, qkr, qkr, jax.ShapeDtypeStruct((batch, DIFF_HEADS, spt, LANES, tm), BF16)],
        scratch_shapes=[pltpu.VMEM((tm + 8, cwp.shape[1]), F32)],
        compiler_params=_params("arbitrary"),
        name="in_proj0",
    )(x, g, w, w_hp[0], w_hp[1], cwp, cos_t, sin_t)


def _mix_ffn_kernel(*refs, trans, tm, tf, tiles_per_batch, final_norm):
    n_in = len(trans)
    a_refs = refs[:n_in]
    w_refs = refs[n_in:2 * n_in]
    x_ref, g_ref, wi_ref, cw_ref, wo_ref, fg_ref, o_ref, gs_ref, act_ref = refs[2 * n_in:]
    i = pl.program_id(0)

    @pl.when(i % tiles_per_batch == 0)
    def _():
        gs_ref[0:8, :] = jnp.zeros((8, D_FF), F32)

    x = x_ref[...]
    for a_ref, w_ref, tr in zip(a_refs, w_refs, trans):
        x = x + (_dot_tn if tr else _dot)(a_ref[...], w_ref[...])
    xn = _rms(x, g_ref[...]).astype(BF16)
    for c0 in range(0, D_FF, tf):
        cs = slice(c0, c0 + tf)
        gate = _dot(xn, wi_ref[:, cs])
        up = _dot(xn, wi_ref[:, D_FF + c0:D_FF + c0 + tf])
        gs_ref[8:, cs] = gate
        cw = cw_ref[:, cs]
        y = (cw[2:3] * gate + cw[1:2] * gs_ref[pl.ds(7, tm), cs]
             + cw[0:1] * gs_ref[pl.ds(6, tm), cs] + cw[3:4])
        act_ref[:, cs] = (_silu(y) * up).astype(BF16)
    gs_ref[0:8, :] = gs_ref[tm:tm + 8, :]
    r = x + _dot(act_ref[...], wo_ref[...])
    o_ref[...] = _rms(r, fg_ref[...]) if final_norm else r


def _mix_ffn(acts, ws, x, g, w_in, cwb, w_out, final_g, layer, seq, tm, tf, final_norm):
    t, d = x.shape
    tpb = seq // tm
    trans = tuple(a.ndim == 3 for a in acts)
    kern = functools.partial(_mix_ffn_kernel, trans=trans, tm=tm, tf=tf, tiles_per_batch=tpb,
                             final_norm=final_norm)

    def of_layer(w):
        return pl.BlockSpec((None,) + w.shape[1:], lambda i: (layer, 0, 0), pipeline_mode=pl.Buffered(1))

    in_specs = ([pl.BlockSpec((None, a.shape[1], tm), lambda i: (i // tpb, 0, i % tpb)) if tr
                 else pl.BlockSpec((tm, a.shape[1]), lambda i: (i, 0)) for a, tr in zip(acts, trans)]
                + [_resident(w.shape) for w in ws]
                + [pl.BlockSpec((tm, d), lambda i: (i, 0)),
                   _resident((1, d)),
                   of_layer(w_in),
                   _resident(cwb.shape),
                   of_layer(w_out),
                   _resident((1, d))])
    return pl.pallas_call(
        kern,
        grid=(t // tm,),
        in_specs=in_specs,
        out_specs=pl.BlockSpec((tm, d), lambda i: (i, 0)),
        out_shape=jax.ShapeDtypeStruct((t, d), F32),
        scratch_shapes=[pltpu.VMEM((tm + 8, D_FF), F32),
                        pltpu.VMEM((tm, D_FF), BF16)],
        compiler_params=_params("arbitrary"),
        name="mix_ffn",
    )(*acts, *ws, x, g, w_in, cwb, w_out, final_g)


def _diff_attn_kernel(qa_ref, qb_ref, k_ref, vt_ref, lam_ref, gain_ref, o_ref, acc_ref, s_ref, *, tq, tk):
    qi = pl.program_id(2)
    acc_ref[...] = jnp.zeros(acc_ref.shape, F32)
    qs = (qa_ref[...], qb_ref[...])
    assert tq == 2 * tk

    def scores(ki, buf, q0=0):
        off = pl.multiple_of(ki * tk, tk)
        kt = k_ref[pl.ds(off, tk), :]
        for mp in range(2):
            s_ref[buf, mp, :, q0:] = _dot_nt(kt, qs[mp][q0:])

    def softmax_pv(ki, buf, carry, mask, q0=0):
        vt = vt_ref[ki]
        new, ps, alphas = [], [], []
        for mp in range(2):
            m_old, l_old = carry[2 * mp], carry[2 * mp + 1]
            s = s_ref[buf, mp, :, q0:]
            if mask is not None:
                s = jnp.where(mask[:, q0:], s, MASKED)
            m_new = jnp.maximum(m_old[:, q0:], jnp.max(s, axis=0, keepdims=True))
            alpha = jnp.exp2(m_old[:, q0:] - m_new)
            p = jnp.exp2(s - m_new)
            l_new = alpha * l_old[:, q0:] + jnp.sum(p, axis=0, keepdims=True)
            ps.append(p.astype(BF16))
            alphas.append(alpha)
            if q0:
                m_new = jnp.concatenate([m_old[:, :q0], m_new], axis=1)
                l_new = jnp.concatenate([l_old[:, :q0], l_new], axis=1)
            new += [m_new, l_new]
        for mp in range(2):
            acc_ref[mp, :, q0:] = alphas[mp] * acc_ref[mp, :, q0:] + _dot(vt, ps[mp])
        return tuple(new)

    row0 = jnp.full((1, tq), MASKED, F32)
    zero = jnp.zeros((1, tq), F32)

    scores(0, 0)

    def body(kj, c):
        scores(2 * kj + 1, 1)
        c = softmax_pv(2 * kj, 0, c, None)
        scores(2 * kj + 2, 0)
        return softmax_pv(2 * kj + 1, 1, c, None)

    carry = lax.fori_loop(0, qi, body, (row0, zero, row0, zero))

    key = lax.broadcasted_iota(jnp.int32, (tk, tq), 0)
    qry = lax.broadcasted_iota(jnp.int32, (tk, tq), 1)
    scores(2 * qi + 1, 1, tk)
    carry = softmax_pv(2 * qi, 0, carry, (key // CHUNK) <= (qry // CHUNK))
    carry = softmax_pv(2 * qi + 1, 1, carry, (key // CHUNK + tk // CHUNK) <= (qry // CHUNK), tk)
    _, l1, _, l2 = carry

    o = acc_ref[0] / l1 - lam_ref[...] * (acc_ref[1] / l2)
    ms = jnp.mean(o * o, axis=0, keepdims=True)
    o_ref[...] = (o * lax.rsqrt(ms + RMS_EPS) * gain_ref[...]).astype(o_ref.dtype)


def _diff_attn(qa, qb, kr, vt, lam_row, gain_t, batch, seq, tq):
    nq = seq // tq
    nk, _, tk = vt.shape[2:]
    kern = functools.partial(_diff_attn_kernel, tq=tq, tk=tk)
    qspec = pl.BlockSpec((tq, LANES), lambda b, h, q: (b * nq + q, h))
    return pl.pallas_call(
        kern,
        grid=(batch, DIFF_HEADS, nq),
        in_specs=[qspec, qspec,
                  pl.BlockSpec((seq, LANES), lambda b, h, q: (b, h)),
                  pl.BlockSpec((None, None, nk, LANES, tk), lambda b, h, q: (b, h, 0, 0, 0)),
                  pl.BlockSpec((1, tq), lambda b, h, q: (0, 0)),
                  pl.BlockSpec((LANES, tq), lambda b, h, q: (0, 0))],
        out_specs=pl.BlockSpec((None, LANES, tq), lambda b, h, q: (b, h, q)),
        out_shape=jax.ShapeDtypeStruct((batch, DIFF_HEADS * LANES, seq), BF16),
        scratch_shapes=[pltpu.VMEM((2, LANES, tq), F32),
                        pltpu.VMEM((2, 2, tk, tq), F32)],
        compiler_params=_params("parallel", "parallel", "arbitrary"),
        name="diff_attn",
    )(qa, qb, kr, vt, lam_row, gain_t)


def _band_attn_kernel(q_ref, kc_ref, kp_ref, vc_ref, vp_ref, bias_ref, o_ref,
                      kw_ref, vw_ref, *, tq, tiles_per_batch):
    i = pl.program_id(0)
    pad = CA_LEFT_CHUNKS * CHUNK
    kw_ref[0:tq, :] = kp_ref[...]
    kw_ref[tq:, :] = kc_ref[...]
    vw_ref[0:tq, :] = vp_ref[...]
    vw_ref[tq:, :] = vc_ref[...]
    has_prev = (i % tiles_per_batch) != 0
    lane = lax.broadcasted_iota(jnp.int32, (BAND_SUB, LANES), 1)
    lo = lane < CA_HD
    col = lax.broadcasted_iota(jnp.int32, (BAND_SUB, BAND_WIN), 1)

    def sub_tile(s, carry, first):
        r0 = pl.multiple_of(s * BAND_SUB, BAND_SUB)
        w0 = pl.multiple_of(r0 + (tq - pad), BAND_SUB)
        css = [slice(pr * LANES, (pr + 1) * LANES) for pr in range(CA_HEADS // 2)]
        scs = []
        for pr, cs in enumerate(css):
            qp = q_ref[pl.ds(r0, BAND_SUB), cs]
            kw = kw_ref[pl.ds(w0, BAND_WIN), cs]
            for hh in range(2):
                qh = jnp.where(lo, qp, 0) if hh == 0 else jnp.where(lo, 0, qp)
                scs.append(_dot_nt(qh.astype(BF16), kw))
        ps, dens = [], []
        for h, sc in enumerate(scs):
            sc = sc + bias_ref[h]
            if first:
                sc = jnp.where(col + w0 >= tq, sc, MASKED)
            p = jnp.exp2(sc - jnp.max(sc, axis=-1, keepdims=True))
            dens.append(jnp.sum(p, axis=-1, keepdims=True))
            ps.append(p.astype(BF16))
        for pr, cs in enumerate(css):
            vw = vw_ref[pl.ds(w0, BAND_WIN), cs]
            outs = [_dot(ps[2 * pr + hh], vw) / dens[2 * pr + hh] for hh in range(2)]
            o_ref[pl.ds(r0, BAND_SUB), cs] = jnp.where(lo, outs[0], outs[1]).astype(o_ref.dtype)
        return carry

    @pl.when(has_prev)
    def _():
        lax.fori_loop(0, tq // BAND_SUB, lambda s, c: sub_tile(s, c, False), 0)

    @pl.when(jnp.logical_not(has_prev))
    def _():
        lax.fori_loop(0, tq // BAND_SUB, lambda s, c: sub_tile(s, c, True), 0)


def _band_attn(qkv, bias, seq, tq):
    t = qkv.shape[0]
    d = CA_HEADS * CA_HD
    tpb = seq // tq
    kern = functools.partial(_band_attn_kernel, tq=tq, tiles_per_batch=tpb)

    def prev(i):
        return jnp.where(i % tpb == 0, i, i - 1)

    return pl.pallas_call(
        kern,
        grid=(t // tq,),
        in_specs=[pl.BlockSpec((tq, d), lambda i: (i, 0)),
                  pl.BlockSpec((tq, d), lambda i: (i, 1)),
                  pl.BlockSpec((tq, d), lambda i: (prev(i), 1)),
                  pl.BlockSpec((tq, d), lambda i: (i, 2)),
                  pl.BlockSpec((tq, d), lambda i: (prev(i), 2)),
                  _resident(bias.shape)],
        out_specs=pl.BlockSpec((tq, d), lambda i: (i, 0)),
        out_shape=jax.ShapeDtypeStruct((t, d), BF16),
        scratch_shapes=[pltpu.VMEM((2 * tq, d), BF16),
                        pltpu.VMEM((2 * tq, d), BF16)],
        compiler_params=_params("parallel"),
        name="band_attn",
    )(qkv, qkv, qkv, qkv, qkv, bias)


def _deltanet_kernel(q_ref, k_ref, v_ref, ab_ref, gate_ref, ea_ref, dtb_ref, ng_ref, o_ref, state_ref, *, tm):
    i = pl.program_id(1)
    pair = 2 * CHUNK

    @pl.when(i == 0)
    def _():
        state_ref[...] = jnp.zeros(state_ref.shape, F32)

    qc = q_ref[...].astype(F32)
    kc = k_ref[...].astype(F32)
    vc = v_ref[...].astype(F32)

    ab = ab_ref[...]
    g_all = -ea_ref[...] * jax.nn.softplus(ab + dtb_ref[...])
    beta_all = jax.nn.sigmoid(ab)

    row = lax.broadcasted_iota(jnp.int32, (pair, pair), 0)
    col = lax.broadcasted_iota(jnp.int32, (pair, pair), 1)
    same = (row // CHUNK) == (col // CHUNK)
    tri = jnp.logical_and(same, col <= row)
    strict = jnp.logical_and(same, col < row)
    lmat = jnp.where(tri, 1.0, 0.0).astype(BF16)
    su = jnp.where(jnp.logical_and(same, row > col), 1.0, 0.0)
    eye = jnp.where(row == col, 1.0, 0.0)
    top = row < CHUNK

    npair = tm // pair
    units = [(h, dc) for h in range(DN_HEADS) for dc in range(npair)]
    nu = len(units)
    hsl = [slice(h * DN_DK, (h + 1) * DN_DK) for h in range(DN_HEADS)]
    rsl = [slice(dc * pair, (dc + 1) * pair) for dc in range(npair)]

    def bf(x):
        return x.astype(BF16)

    qn, kn, g_b, beta_b = [], [], [], []
    for h in range(DN_HEADS):
        qh = qc[:, hsl[h]]
        kh = kc[:, hsl[h]]
        qn.append(qh * lax.rsqrt(jnp.sum(qh * qh, axis=-1, keepdims=True) + RMS_EPS) * (DN_DK ** -0.5))
        kn.append(kh * lax.rsqrt(jnp.sum(kh * kh, axis=-1, keepdims=True) + RMS_EPS))
        g_b.append(jnp.broadcast_to(g_all[:, h:h + 1], (tm, pair)))
        beta_b.append(jnp.broadcast_to(beta_all[:, DN_HEADS + h:DN_HEADS + h + 1], (tm, pair)))
    q2 = [qn[h][rsl[dc]] for h, dc in units]
    k2 = [kn[h][rsl[dc]] for h, dc in units]
    v2 = [vc[rsl[dc], hsl[h]] for h, dc in units]
    b2 = [beta_b[h][rsl[dc]] for h, dc in units]
    k2b = [bf(x) for x in k2]

    g_cat = jnp.concatenate([g_b[h][rsl[dc]] for h, dc in units], axis=1)
    gcum_cat = _dot_exact_lhs(lmat, g_cat)
    gdiff_cat = _dot_exact_lhs(lmat, g_cat * jnp.concatenate([su] * nu, axis=1))
    usl = [slice(u * pair, (u + 1) * pair) for u in range(nu)]
    gcum = [gcum_cat[:, s] for s in usl]
    decay = [jnp.where(tri, jnp.exp(jnp.where(tri, gdiff_cat[:, s], 0.0)), 0.0) for s in usl]
    kk = [_dot_nt(x, x) for x in k2b]
    a = [jnp.where(strict, b2[u] * kk[u] * decay[u], 0.0) for u in range(nu)]
    tinv = [eye - x for x in a]
    pw = a
    for _ in range(5):
        pwb = [bf(x) for x in pw]
        pw = [_dot(x, x) for x in pwb]
        pwb = [bf(x) for x in pw]
        tinv = [tinv[u] + _dot(bf(tinv[u]), pwb[u]) for u in range(nu)]
    eg = [jnp.exp(x) for x in gcum]
    rhs = [jnp.concatenate([v2[u] * b2[u], k2[u] * b2[u] * eg[u]], axis=1) for u in range(nu)]
    uw = [_dot(bf(tinv[u]), bf(rhs[u])) for u in range(nu)]
    qk = [jnp.where(tri, _dot_nt(bf(q2[u]), k2b[u]) * decay[u], 0.0) for u in range(nu)]
    qd = [bf(q2[u] * eg[u]) for u in range(nu)]
    gl = [(gcum[u][CHUNK - 1:CHUNK], gcum[u][pair - 1:pair]) for u in range(nu)]
    kd = [bf(k2[u] * jnp.exp(jnp.where(top, gl[u][0], gl[u][1]) - gcum[u])) for u in range(nu)]
    egl = [(jnp.exp(gl[u][0]), jnp.exp(gl[u][1])) for u in range(nu)]

    st = [state_ref[h] for h in range(DN_HEADS)]
    vns = [[None, None] for _ in range(nu)]
    ois = [[None, None] for _ in range(nu)]
    for dc in range(npair):
        for c in range(2):
            cr = slice(c * CHUNK, (c + 1) * CHUNK)
            us = [h * npair + dc for h in range(DN_HEADS)]
            stb = [bf(x) for x in st]
            vn = [uw[u][cr, :DN_DK] - _dot(bf(uw[u][cr, DN_DK:]), stb[h]) for h, u in enumerate(us)]
            for h, u in enumerate(us):
                ois[u][c] = _dot(qd[u][cr], stb[h])
                vns[u][c] = vn[h]
            st = [st[h] * egl[u][c] + _dot_tn(kd[u][cr], bf(vn[h])) for h, u in enumerate(us)]
    for h in range(DN_HEADS):
        state_ref[h] = st[h]

    for u, (h, dc) in enumerate(units):
        vn2 = jnp.concatenate(vns[u], axis=0)
        o = jnp.concatenate(ois[u], axis=0) + _dot(bf(qk[u]), bf(vn2))
        o = _rms(o, ng_ref[...]) * _silu(gate_ref[rsl[dc], hsl[h]].astype(F32))
        o_ref[rsl[dc], hsl[h]] = o.astype(o_ref.dtype)


def _deltanet(p0, ab, ea, dtb, ng, batch, seq, tm):
    t = p0.shape[0]
    w = DN_QK
    nt = seq // tm
    kern = functools.partial(_deltanet_kernel, tm=tm)

    def cur(cb):
        return pl.BlockSpec((tm, w), lambda b, i: (b * nt + i, cb))

    vec = pl.BlockSpec((1, LANES), lambda b, i: (0, 0))
    return pl.pallas_call(
        kern,
        grid=(batch, nt),
        in_specs=[cur(0), cur(1), cur(2),
                  pl.BlockSpec((tm, LANES), lambda b, i: (b * nt + i, 0)),
                  cur(3), vec, vec, vec],
        out_specs=pl.BlockSpec((tm, w), lambda b, i: (b * nt + i, 0)),
        out_shape=jax.ShapeDtypeStruct((t, w), BF16),
        scratch_shapes=[pltpu.VMEM((DN_HEADS, DN_DK, DN_DK), F32)],
        compiler_params=_params("arbitrary", "arbitrary"),
        name="deltanet",
    )(p0, p0, p0, ab, p0, ea, dtb, ng)


def _tile(n, want):
    return min(n, want)


def _pad_rows(a, rows):
    return jnp.concatenate([a, jnp.zeros((rows - a.shape[0], a.shape[1]), a.dtype)], axis=0)


def kernel(x, even_norm_mix, even_w_in, even_dn_conv, even_dn_A_log, even_dn_dt_bias, even_dn_norm,
           even_diff_lambda, even_diff_subln, even_w_out, odd_norm_mix, odd_w_qkv, odd_rel_bias,
           odd_w_out, ffn_norm, ffn_w_in, ffn_conv_w, ffn_conv_b, ffn_w_out, final_norm):
    batch, seq, d = x.shape
    t = batch * seq
    xr = x.reshape(t, d)
    tm = _tile(seq, 512)

    ffn_w_in_b = ffn_w_in.astype(BF16)
    ffn_w_out_b = ffn_w_out.astype(BF16)

    def mix_ffn(acts, ws, xr, i, final):
        cwb = _pad_rows(jnp.concatenate([ffn_conv_w[i], ffn_conv_b[i][None, :]], axis=0), 8)
        return _mix_ffn(acts, ws, xr, ffn_norm[i][None, :], ffn_w_in_b, cwb, ffn_w_out_b,
                        final_norm[None, :], i, seq, tm, 256, final)

    w0 = even_w_in[0]
    n_qkv = 3 * DN_QK
    w_main = jnp.concatenate([w0[:, :n_qkv], w0[:, n_qkv + 2 * DN_HEADS:]], axis=1).astype(BF16)
    w_ab = jnp.pad(w0[:, n_qkv:n_qkv + 2 * DN_HEADS], ((0, 0), (0, LANES - 2 * DN_HEADS)))
    w_ab_hi = w_ab.astype(BF16)
    w_ab_lo = (w_ab - w_ab_hi.astype(F32)).astype(BF16)
    g0 = even_norm_mix[0][None, :]
    half = DIFF_HD // 2
    inv = 1.0 / (ROPE_THETA ** (jnp.arange(0, DIFF_HD, 2, dtype=F32) / DIFF_HD))
    ang = jnp.arange(seq, dtype=F32)[:, None] * inv[None, :]
    cos_t = jnp.tile(jnp.cos(ang), (1, LANES // half))
    sign = jnp.tile(jnp.concatenate([-jnp.ones((half,), F32), jnp.ones((half,), F32)]), LANES // DIFF_HD)
    sin_t = jnp.tile(jnp.sin(ang), (1, LANES // half)) * sign[None, :]
    p0, ab, qa, qb, kr, vt = _in_proj0(xr, g0, w_main, (w_ab_hi, w_ab_lo), _pad_rows(even_dn_conv[0], 8),
                                       cos_t, sin_t, batch, seq, tm, 512)

    def lane_vec(v):
        return jnp.pad(v.astype(F32), (0, LANES - v.shape[0]))[None, :]

    o_a = _deltanet(p0, ab, lane_vec(jnp.exp(even_dn_A_log[0])), lane_vec(even_dn_dt_bias[0]),
                    even_dn_norm[0][None, :], batch, seq, _tile(seq, 512))

    lam_init = 0.8 - 0.6 * math.exp(-0.3 * 0)
    lp = even_diff_lambda[0].astype(F32)
    lam = jnp.exp(jnp.sum(lp[0] * lp[1])) - jnp.exp(jnp.sum(lp[2] * lp[3])) + lam_init
    tq = _tile(seq, 1024)
    lam_row = jnp.full((1, tq), lam, F32)
    gain_t = jnp.broadcast_to((even_diff_subln[0] * (1.0 - lam_init))[:, None], (LANES, tq))
    o_b = _diff_attn(qa, qb, kr, vt, lam_row, gain_t, batch, seq, tq)

    wo = even_w_out[0].astype(BF16)
    xr = mix_ffn([o_a, o_b], [wo[:DN_QK], wo[DN_QK:]], xr, 0, False)

    wq = odd_w_qkv[0]
    log2e = math.log2(math.e)
    wq = jnp.concatenate([wq[:, :D_MODEL] * (CA_HD ** -0.5 * log2e), wq[:, D_MODEL:]], axis=1).astype(BF16)
    qkv = _norm_matmul(xr, odd_norm_mix[0][None, :], wq, tm, 512)
    pad = CA_LEFT_CHUNKS * CHUNK
    qi = jnp.arange(BAND_SUB)[:, None]
    kj = jnp.arange(BAND_WIN)[None, :]
    in_band = jnp.logical_and(kj // CHUNK >= qi // CHUNK, kj // CHUNK <= qi // CHUNK + CA_LEFT_CHUNKS)
    m_len = BAND_WIN + BAND_SUB + 1
    rel = jnp.arange(m_len) - BAND_SUB - pad
    e = odd_rel_bias[0].astype(F32)[:, jnp.clip(rel, -REL_CLIP, REL_CLIP) + REL_CLIP]
    bias = jnp.tile(e, (1, BAND_SUB))[:, :BAND_SUB * (m_len - 1)].reshape(CA_HEADS, BAND_SUB, m_len - 1)
    bias = jnp.where(in_band[None], bias[:, :, BAND_SUB:BAND_SUB + BAND_WIN] * log2e, MASKED)
    o_c = _band_attn(qkv, bias, seq, tm)
    xr = mix_ffn([o_c], [odd_w_out[0].astype(BF16)], xr, 1, True)
    return xr.reshape(batch, seq, d)
```

```python
import functools
import math

import jax
import jax.numpy as jnp
from jax import lax
from jax.experimental import pallas as pl
from jax.experimental.pallas import tpu as pltpu

F32 = jnp.float32
BF16 = jnp.bfloat16

D_MODEL = 1024
CHUNK = 64
RMS_EPS = 1e-6
ROPE_THETA = 10000.0
DN_HEADS = 4
DN_DK = 128
DN_QK = 512
DN_CONV = 4
DIFF_HEADS = 4
DIFF_HD = 64
CA_HEADS = 16
CA_HD = 64
CA_LEFT_CHUNKS = 8
REL_CLIP = 256
D_FF = 2816
MASKED = -1e30
VMEM_LIMIT = 56 * 1024 * 1024
LANES = 128
BAND_SUB = 128
BAND_WIN = BAND_SUB + CA_LEFT_CHUNKS * CHUNK

def _params(*sem):
    return pltpu.CompilerParams(dimension_semantics=sem, vmem_limit_bytes=VMEM_LIMIT)


def _rms(x, g):
    ms = jnp.mean(x * x, axis=-1, keepdims=True)
    return x * lax.rsqrt(ms + RMS_EPS) * g


def _silu(x):
    return x * jax.nn.sigmoid(x)


def _dot(a, b):
    return jnp.dot(a, b, preferred_element_type=F32)


def _dot_nt(a, b):
    return lax.dot_general(a, b, (((1,), (1,)), ((), ())), preferred_element_type=F32)


def _dot_tn(a, b):
    return lax.dot_general(a, b, (((0,), (0,)), ((), ())), preferred_element_type=F32)


def _dot_exact_lhs(l_bf16, x):
    x1 = x.astype(BF16)
    x2 = (x - x1.astype(F32)).astype(BF16)
    return _dot(l_bf16, x1) + _dot(l_bf16, x2)


def _resident(shape):
    return pl.BlockSpec(shape, lambda *_: (0,) * len(shape), pipeline_mode=pl.Buffered(1))


def _norm_matmul_kernel(x_ref, g_ref, w_ref, o_ref, *, tn):
    xn = _rms(x_ref[...], g_ref[...]).astype(BF16)
    for n0 in range(0, w_ref.shape[1], tn):
        o_ref[:, n0:n0 + tn] = _dot(xn, w_ref[:, n0:n0 + tn]).astype(o_ref.dtype)


def _norm_matmul(x, g, w, tm, tn):
    t, d = x.shape
    n = w.shape[1]
    return pl.pallas_call(
        functools.partial(_norm_matmul_kernel, tn=tn),
        grid=(t // tm,),
        in_specs=[pl.BlockSpec((tm, d), lambda i: (i, 0)), _resident((1, d)), _resident(w.shape)],
        out_specs=pl.BlockSpec((tm, n), lambda i: (i, 0)),
        out_shape=jax.ShapeDtypeStruct((t, n), BF16),
        compiler_params=_params("parallel"),
        name="norm_matmul",
    )(x, g, w)


def _in_proj0_kernel(x_ref, g_ref, w_ref, whi_ref, wlo_ref, cw_ref, cos_ref, sin_ref,
                     dn_ref, ab_ref, qa_ref, qb_ref, kr_ref, vt_ref, xx_ref, *, tm, tn, tiles_per_batch):
    i = pl.program_id(0)
    n_conv = cw_ref.shape[1]

    @pl.when(i % tiles_per_batch == 0)
    def _():
        xx_ref[0:8, :] = jnp.zeros((8, n_conv), F32)

    xn = _rms(x_ref[...], g_ref[...])
    hi = xn.astype(BF16)
    lo_x = (xn - hi.astype(F32)).astype(BF16)
    ab_ref[...] = _dot(hi, whi_ref[...]) + (_dot(hi, wlo_ref[...]) + _dot(lo_x, whi_ref[...]))
    n_dn = dn_ref.shape[1]
    for n0 in range(0, n_dn, tn):
        cs = slice(n0, n0 + tn)
        if n0 < n_conv:
            xx_ref[8:, cs] = _dot(hi, w_ref[:, cs])
        else:
            dn_ref[:, cs] = _dot(hi, w_ref[:, cs]).astype(BF16)
    w = DIFF_HEADS * LANES
    fq = _dot(hi, w_ref[:, n_dn:n_dn + w])
    fk = _dot(hi, w_ref[:, n_dn + w:n_dn + 2 * w])
    fv = _dot(hi, w_ref[:, n_dn + 2 * w:n_dn + 3 * w])

    for n0 in range(0, n_conv, tn):
        cs = slice(n0, n0 + tn)
        cw = cw_ref[:, cs]
        y = cw[DN_CONV - 1:DN_CONV] * xx_ref[pl.ds(8, tm), cs]
        for s in range(1, DN_CONV):
            y = y + cw[DN_CONV - 1 - s:DN_CONV - s] * xx_ref[pl.ds(8 - s, tm), cs]
        dn_ref[:, cs] = _silu(y).astype(BF16)
    xx_ref[0:8, :] = xx_ref[tm:tm + 8, :]

    cos = cos_ref[...]
    sin = sin_ref[...]
    lane = lax.broadcasted_iota(jnp.int32, cos.shape, 1)
    first = (lane % DIFF_HD) < (DIFF_HD // 2)
    lo = lane < DIFF_HD

    def rope(x):
        rot = jnp.where(first, pltpu.roll(x, LANES - DIFF_HD // 2, 1), pltpu.roll(x, DIFF_HD // 2, 1))
        return x * cos + rot * sin

    for h in range(DIFF_HEADS):
        hs = slice(h * LANES, (h + 1) * LANES)
        q = rope(fq[:, hs]) * (DIFF_HD ** -0.5 * math.log2(math.e))
        qa_ref[:, hs] = jnp.where(lo, q, 0.0).astype(BF16)
        qb_ref[:, hs] = jnp.where(lo, 0.0, q).astype(BF16)
        kr_ref[:, hs] = rope(fk[:, hs]).astype(BF16)
        vt_ref[h] = fv[:, hs].T.astype(BF16)


def _in_proj0(x, g, w, w_hp, cwp, cos_t, sin_t, batch, seq, tm, tn):
    t, d = x.shape
    w_diff = DIFF_HEADS * LANES
    n_dn = w.shape[1] - 3 * w_diff
    spt = seq // tm
    row = lambda n: pl.BlockSpec((tm, n), lambda i: (i, 0))
    tab = pl.BlockSpec((tm, LANES), lambda i: (i % spt, 0))
    qkr = jax.ShapeDtypeStruct((t, w_diff), BF16)
    return pl.pallas_call(
        functools.partial(_in_proj0_kernel, tm=tm, tn=tn, tiles_per_batch=spt),
        grid=(t // tm,),
        in_specs=[row(d), _resident((1, d)), _resident(w.shape), _resident(w_hp[0].shape),
                  _resident(w_hp[1].shape), _resident(cwp.shape), tab, tab],
        out_specs=[row(n_dn), row(w_hp[0].shape[1]), row(w_diff), row(w_diff), row(w_diff),
                   pl.BlockSpec((None, DIFF_HEADS, None, LANES, tm), lambda i: (i // spt, 0, i % spt, 0, 0))],
        out_shape=[jax.ShapeDtypeStruct((t, n_dn), BF16), jax.ShapeDtypeStruct((t, w_hp[0].shape[1]), F32),
                   qkr, qkr, qkr, jax.ShapeDtypeStruct((batch, DIFF_HEADS, spt, LANES, tm), BF16)],
        scratch_shapes=[pltpu.VMEM((tm + 8, cwp.shape[1]), F32)],
        compiler_params=_params("arbitrary"),
        name="in_proj0",
    )(x, g, w, w_hp[0], w_hp[1], cwp, cos_t, sin_t)


def _mix_ffn_kernel(*refs, trans, tm, tf, tiles_per_batch, final_norm):
    n_in = len(trans)
    a_refs = refs[:n_in]
    w_refs = refs[n_in:2 * n_in]
    x_ref, g_ref, wi_ref, cw_ref, wo_ref, fg_ref, o_ref, gs_ref, act_ref = refs[2 * n_in:]
    i = pl.program_id(0)

    @pl.when(i % tiles_per_batch == 0)
    def _():
        gs_ref[0:8, :] = jnp.zeros((8, D_FF), F32)

    x = x_ref[...]
    for a_ref, w_ref, tr in zip(a_refs, w_refs, trans):
        x = x + (_dot_tn if tr else _dot)(a_ref[...], w_ref[...])
    xn = _rms(x, g_ref[...]).astype(BF16)
    for c0 in range(0, D_FF, tf):
        cs = slice(c0, c0 + tf)
        gate = _dot(xn, wi_ref[:, cs])
        up = _dot(xn, wi_ref[:, D_FF + c0:D_FF + c0 + tf])
        gs_ref[8:, cs] = gate
        cw = cw_ref[:, cs]
        y = (cw[2:3] * gate + cw[1:2] * gs_ref[pl.ds(7, tm), cs]
             + cw[0:1] * gs_ref[pl.ds(6, tm), cs] + cw[3:4])
        act_ref[:, cs] = (_silu(y) * up).astype(BF16)
    gs_ref[0:8, :] = gs_ref[tm:tm + 8, :]
    r = x + _dot(act_ref[...], wo_ref[...])
    o_ref[...] = _rms(r, fg_ref[...]) if final_norm else r


def _mix_ffn(acts, ws, x, g, w_in, cwb, w_out, final_g, layer, seq, tm, tf, final_norm):
    t, d = x.shape
    tpb = seq // tm
    trans = tuple(a.ndim == 3 for a in acts)
    kern = functools.partial(_mix_ffn_kernel, trans=trans, tm=tm, tf=tf, tiles_per_batch=tpb,
                             final_norm=final_norm)

    def of_layer(w):
        return pl.BlockSpec((None,) + w.shape[1:], lambda i: (layer, 0, 0), pipeline_mode=pl.Buffered(1))

    in_specs = ([pl.BlockSpec((None, a.shape[1], tm), lambda i: (i // tpb, 0, i % tpb)) if tr
                 else pl.BlockSpec((tm, a.shape[1]), lambda i: (i, 0)) for a, tr in zip(acts, trans)]
                + [_resident(w.shape) for w in ws]
                + [pl.BlockSpec((tm, d), lambda i: (i, 0)),
                   _resident((1, d)),
                   of_layer(w_in),
                   _resident(cwb.shape),
                   of_layer(w_out),
                   _resident((1, d))])
    return pl.pallas_call(
        kern,
        grid=(t // tm,),
        in_specs=in_specs,
        out_specs=pl.BlockSpec((tm, d), lambda i: (i, 0)),
        out_shape=jax.ShapeDtypeStruct((t, d), F32),
        scratch_shapes=[pltpu.VMEM((tm + 8, D_FF), F32),
                        pltpu.VMEM((tm, D_FF), BF16)],
        compiler_params=_params("arbitrary"),
        name="mix_ffn",
    )(*acts, *ws, x, g, w_in, cwb, w_out, final_g)


def _diff_attn_kernel(qa_ref, qb_ref, k_ref, vt_ref, lam_ref, gain_ref, o_ref, acc_ref, s_ref, *, tq, tk):
    qi = pl.program_id(2)
    acc_ref[...] = jnp.zeros(acc_ref.shape, F32)
    qs = (qa_ref[...], qb_ref[...])
    n_sub = tq // tk
    assert n_sub % 2 == 0

    def scores(ki, buf, q0=0):
        off = pl.multiple_of(ki * tk, tk)
        kt = k_ref[pl.ds(off, tk), :]
        for mp in range(2):
            s_ref[buf, mp, :, q0:] = _dot_nt(kt, qs[mp][q0:])

    def softmax_pv(ki, buf, carry, mask, q0=0):
        vt = vt_ref[ki]
        new, ps, alphas = [], [], []
        for mp in range(2):
            m_old, l_old = carry[2 * mp], carry[2 * mp + 1]
            s = s_ref[buf, mp, :, q0:]
            if mask is not None:
                s = jnp.where(mask[:, q0:], s, MASKED)
            m_new = jnp.maximum(m_old[:, q0:], jnp.max(s, axis=0, keepdims=True))
            alpha = jnp.exp2(m_old[:, q0:] - m_new)
            p = jnp.exp2(s - m_new)
            l_new = alpha * l_old[:, q0:] + jnp.sum(p, axis=0, keepdims=True)
            ps.append(p.astype(BF16))
            alphas.append(alpha)
            if q0:
                m_new = jnp.concatenate([m_old[:, :q0], m_new], axis=1)
                l_new = jnp.concatenate([l_old[:, :q0], l_new], axis=1)
            new += [m_new, l_new]
        for mp in range(2):
            acc_ref[mp, :, q0:] = alphas[mp] * acc_ref[mp, :, q0:] + _dot(vt, ps[mp])
        return tuple(new)

    row0 = jnp.full((1, tq), MASKED, F32)
    zero = jnp.zeros((1, tq), F32)

    scores(0, 0)

    def body(kj, c):
        scores(2 * kj + 1, 1)
        c = softmax_pv(2 * kj, 0, c, None)
        scores(2 * kj + 2, 0)
        return softmax_pv(2 * kj + 1, 1, c, None)

    carry = lax.fori_loop(0, qi * (n_sub // 2), body, (row0, zero, row0, zero))

    key = lax.broadcasted_iota(jnp.int32, (tk, tq), 0)
    qry = lax.broadcasted_iota(jnp.int32, (tk, tq), 1)
    for d in range(n_sub):
        if d + 1 < n_sub:
            scores(n_sub * qi + d + 1, (d + 1) % 2, (d + 1) * tk)
        carry = softmax_pv(n_sub * qi + d, d % 2, carry, (key // CHUNK + d * (tk // CHUNK)) <= (qry // CHUNK),
                           d * tk)
    _, l1, _, l2 = carry

    o = acc_ref[0] / l1 - lam_ref[...] * (acc_ref[1] / l2)
    ms = jnp.mean(o * o, axis=0, keepdims=True)
    o_ref[...] = (o * lax.rsqrt(ms + RMS_EPS) * gain_ref[...]).astype(o_ref.dtype)


def _diff_attn(qa, qb, kr, vt, lam_row, gain_t, batch, seq, tq):
    nq = seq // tq
    nk, _, tk = vt.shape[2:]
    kern = functools.partial(_diff_attn_kernel, tq=tq, tk=tk)
    qspec = pl.BlockSpec((tq, LANES), lambda b, h, q: (b * nq + q, h))
    return pl.pallas_call(
        kern,
        grid=(batch, DIFF_HEADS, nq),
        in_specs=[qspec, qspec,
                  pl.BlockSpec((seq, LANES), lambda b, h, q: (b, h)),
                  pl.BlockSpec((None, None, nk, LANES, tk), lambda b, h, q: (b, h, 0, 0, 0)),
                  pl.BlockSpec((1, tq), lambda b, h, q: (0, 0)),
                  pl.BlockSpec((LANES, tq), lambda b, h, q: (0, 0))],
        out_specs=pl.BlockSpec((None, LANES, tq), lambda b, h, q: (b, h, q)),
        out_shape=jax.ShapeDtypeStruct((batch, DIFF_HEADS * LANES, seq), BF16),
        scratch_shapes=[pltpu.VMEM((2, LANES, tq), F32),
                        pltpu.VMEM((2, 2, tk, tq), F32)],
        compiler_params=_params("parallel", "parallel", "arbitrary"),
        name="diff_attn",
    )(qa, qb, kr, vt, lam_row, gain_t)


def _band_attn_kernel(q_ref, kc_ref, kp_ref, vc_ref, vp_ref, bias_ref, o_ref,
                      kw_ref, vw_ref, *, tq, tiles_per_batch):
    i = pl.program_id(0)
    pad = CA_LEFT_CHUNKS * CHUNK
    kw_ref[0:tq, :] = kp_ref[...]
    kw_ref[tq:, :] = kc_ref[...]
    vw_ref[0:tq, :] = vp_ref[...]
    vw_ref[tq:, :] = vc_ref[...]
    has_prev = (i % tiles_per_batch) != 0
    lane = lax.broadcasted_iota(jnp.int32, (BAND_SUB, LANES), 1)
    lo = lane < CA_HD
    col = lax.broadcasted_iota(jnp.int32, (BAND_SUB, BAND_WIN), 1)

    def sub_tile(s, carry, first):
        r0 = pl.multiple_of(s * BAND_SUB, BAND_SUB)
        w0 = pl.multiple_of(r0 + (tq - pad), BAND_SUB)
        css = [slice(pr * LANES, (pr + 1) * LANES) for pr in range(CA_HEADS // 2)]
        scs = []
        for pr, cs in enumerate(css):
            qp = q_ref[pl.ds(r0, BAND_SUB), cs]
            kw = kw_ref[pl.ds(w0, BAND_WIN), cs]
            for hh in range(2):
                qh = jnp.where(lo, qp, 0) if hh == 0 else jnp.where(lo, 0, qp)
                scs.append(_dot_nt(qh.astype(BF16), kw))
        ps, dens = [], []
        for h, sc in enumerate(scs):
            sc = sc + bias_ref[h]
            if first:
                sc = jnp.where(col + w0 >= tq, sc, MASKED)
            p = jnp.exp2(sc - jnp.max(sc, axis=-1, keepdims=True))
            dens.append(jnp.sum(p, axis=-1, keepdims=True))
            ps.append(p.astype(BF16))
        for pr, cs in enumerate(css):
            vw = vw_ref[pl.ds(w0, BAND_WIN), cs]
            outs = [_dot(ps[2 * pr + hh], vw) / dens[2 * pr + hh] for hh in range(2)]
            o_ref[pl.ds(r0, BAND_SUB), cs] = jnp.where(lo, outs[0], outs[1]).astype(o_ref.dtype)
        return carry

    @pl.when(has_prev)
    def _():
        lax.fori_loop(0, tq // BAND_SUB, lambda s, c: sub_tile(s, c, False), 0)

    @pl.when(jnp.logical_not(has_prev))
    def _():
        lax.fori_loop(0, tq // BAND_SUB, lambda s, c: sub_tile(s, c, True), 0)


def _band_attn(qkv, bias, seq, tq):
    t = qkv.shape[0]
    d = CA_HEADS * CA_HD
    tpb = seq // tq
    kern = functools.partial(_band_attn_kernel, tq=tq, tiles_per_batch=tpb)

    def prev(i):
        return jnp.where(i % tpb == 0, i, i - 1)

    return pl.pallas_call(
        kern,
        grid=(t // tq,),
        in_specs=[pl.BlockSpec((tq, d), lambda i: (i, 0)),
                  pl.BlockSpec((tq, d), lambda i: (i, 1)),
                  pl.BlockSpec((tq, d), lambda i: (prev(i), 1)),
                  pl.BlockSpec((tq, d), lambda i: (i, 2)),
                  pl.BlockSpec((tq, d), lambda i: (prev(i), 2)),
                  _resident(bias.shape)],
        out_specs=pl.BlockSpec((tq, d), lambda i: (i, 0)),
        out_shape=jax.ShapeDtypeStruct((t, d), BF16),
        scratch_shapes=[pltpu.VMEM((2 * tq, d), BF16),
                        pltpu.VMEM((2 * tq, d), BF16)],
        compiler_params=_params("parallel"),
        name="band_attn",
    )(qkv, qkv, qkv, qkv, qkv, bias)


def _deltanet_kernel(q_ref, k_ref, v_ref, ab_ref, gate_ref, ea_ref, dtb_ref, ng_ref, o_ref, state_ref, *, tm):
    i = pl.program_id(1)
    pair = 2 * CHUNK

    @pl.when(i == 0)
    def _():
        state_ref[...] = jnp.zeros(state_ref.shape, F32)

    qc = q_ref[...].astype(F32)
    kc = k_ref[...].astype(F32)
    vc = v_ref[...].astype(F32)

    ab = ab_ref[...]
    g_all = -ea_ref[...] * jax.nn.softplus(ab + dtb_ref[...])
    beta_all = jax.nn.sigmoid(ab)

    row = lax.broadcasted_iota(jnp.int32, (pair, pair), 0)
    col = lax.broadcasted_iota(jnp.int32, (pair, pair), 1)
    same = (row // CHUNK) == (col // CHUNK)
    tri = jnp.logical_and(same, col <= row)
    strict = jnp.logical_and(same, col < row)
    lmat = jnp.where(tri, 1.0, 0.0).astype(BF16)
    su = jnp.where(jnp.logical_and(same, row > col), 1.0, 0.0)
    eye = jnp.where(row == col, 1.0, 0.0)
    top = row < CHUNK

    npair = tm // pair
    units = [(h, dc) for h in range(DN_HEADS) for dc in range(npair)]
    nu = len(units)
    hsl = [slice(h * DN_DK, (h + 1) * DN_DK) for h in range(DN_HEADS)]
    rsl = [slice(dc * pair, (dc + 1) * pair) for dc in range(npair)]

    def bf(x):
        return x.astype(BF16)

    qn, kn, g_b, beta_b = [], [], [], []
    for h in range(DN_HEADS):
        qh = qc[:, hsl[h]]
        kh = kc[:, hsl[h]]
        qn.append(qh * lax.rsqrt(jnp.sum(qh * qh, axis=-1, keepdims=True) + RMS_EPS) * (DN_DK ** -0.5))
        kn.append(kh * lax.rsqrt(jnp.sum(kh * kh, axis=-1, keepdims=True) + RMS_EPS))
        g_b.append(jnp.broadcast_to(g_all[:, h:h + 1], (tm, pair)))
        beta_b.append(jnp.broadcast_to(beta_all[:, DN_HEADS + h:DN_HEADS + h + 1], (tm, pair)))
    q2 = [qn[h][rsl[dc]] for h, dc in units]
    k2 = [kn[h][rsl[dc]] for h, dc in units]
    v2 = [vc[rsl[dc], hsl[h]] for h, dc in units]
    b2 = [beta_b[h][rsl[dc]] for h, dc in units]
    k2b = [bf(x) for x in k2]

    g_cat = jnp.concatenate([g_b[h][rsl[dc]] for h, dc in units], axis=1)
    gcum_cat = _dot_exact_lhs(lmat, g_cat)
    gdiff_cat = _dot_exact_lhs(lmat, g_cat * jnp.concatenate([su] * nu, axis=1))
    usl = [slice(u * pair, (u + 1) * pair) for u in range(nu)]
    gcum = [gcum_cat[:, s] for s in usl]
    decay = [jnp.where(tri, jnp.exp(jnp.where(tri, gdiff_cat[:, s], 0.0)), 0.0) for s in usl]
    kk = [_dot_nt(x, x) for x in k2b]
    a = [jnp.where(strict, b2[u] * kk[u] * decay[u], 0.0) for u in range(nu)]
    tinv = [eye - x for x in a]
    pw = a
    for _ in range(5):
        pwb = [bf(x) for x in pw]
        pw = [_dot(x, x) for x in pwb]
        pwb = [bf(x) for x in pw]
        tinv = [tinv[u] + _dot(bf(tinv[u]), pwb[u]) for u in range(nu)]
    eg = [jnp.exp(x) for x in gcum]
    rhs = [jnp.concatenate([v2[u] * b2[u], k2[u] * b2[u] * eg[u]], axis=1) for u in range(nu)]
    uw = [_dot(bf(tinv[u]), bf(rhs[u])) for u in range(nu)]
    qk = [jnp.where(tri, _dot_nt(bf(q2[u]), k2b[u]) * decay[u], 0.0) for u in range(nu)]
    qd = [bf(q2[u] * eg[u]) for u in range(nu)]
    gl = [(gcum[u][CHUNK - 1:CHUNK], gcum[u][pair - 1:pair]) for u in range(nu)]
    kd = [bf(k2[u] * jnp.exp(jnp.where(top, gl[u][0], gl[u][1]) - gcum[u])) for u in range(nu)]
    egl = [(jnp.exp(gl[u][0]), jnp.exp(gl[u][1])) for u in range(nu)]

    st = [state_ref[h] for h in range(DN_HEADS)]
    vns = [[None, None] for _ in range(nu)]
    ois = [[None, None] for _ in range(nu)]
    for dc in range(npair):
        for c in range(2):
            cr = slice(c * CHUNK, (c + 1) * CHUNK)
            us = [h * npair + dc for h in range(DN_HEADS)]
            stb = [bf(x) for x in st]
            vn = [uw[u][cr, :DN_DK] - _dot(bf(uw[u][cr, DN_DK:]), stb[h]) for h, u in enumerate(us)]
            for h, u in enumerate(us):
                ois[u][c] = _dot(qd[u][cr], stb[h])
                vns[u][c] = vn[h]
            st = [st[h] * egl[u][c] + _dot_tn(kd[u][cr], bf(vn[h])) for h, u in enumerate(us)]
    for h in range(DN_HEADS):
        state_ref[h] = st[h]

    for u, (h, dc) in enumerate(units):
        vn2 = jnp.concatenate(vns[u], axis=0)
        o = jnp.concatenate(ois[u], axis=0) + _dot(bf(qk[u]), bf(vn2))
        o = _rms(o, ng_ref[...]) * _silu(gate_ref[rsl[dc], hsl[h]].astype(F32))
        o_ref[rsl[dc], hsl[h]] = o.astype(o_ref.dtype)


def _deltanet(p0, ab, ea, dtb, ng, batch, seq, tm):
    t = p0.shape[0]
    w = DN_QK
    nt = seq // tm
    kern = functools.partial(_deltanet_kernel, tm=tm)

    def cur(cb):
        return pl.BlockSpec((tm, w), lambda b, i: (b * nt + i, cb))

    vec = pl.BlockSpec((1, LANES), lambda b, i: (0, 0))
    return pl.pallas_call(
        kern,
        grid=(batch, nt),
        in_specs=[cur(0), cur(1), cur(2),
                  pl.BlockSpec((tm, LANES), lambda b, i: (b * nt + i, 0)),
                  cur(3), vec, vec, vec],
        out_specs=pl.BlockSpec((tm, w), lambda b, i: (b * nt + i, 0)),
        out_shape=jax.ShapeDtypeStruct((t, w), BF16),
        scratch_shapes=[pltpu.VMEM((DN_HEADS, DN_DK, DN_DK), F32)],
        compiler_params=_params("arbitrary", "arbitrary"),
        name="deltanet",
    )(p0, p0, p0, ab, p0, ea, dtb, ng)


def _tile(n, want):
    return min(n, want)


def _pad_rows(a, rows):
    return jnp.concatenate([a, jnp.zeros((rows - a.shape[0], a.shape[1]), a.dtype)], axis=0)


def kernel(x, even_norm_mix, even_w_in, even_dn_conv, even_dn_A_log, even_dn_dt_bias, even_dn_norm,
           even_diff_lambda, even_diff_subln, even_w_out, odd_norm_mix, odd_w_qkv, odd_rel_bias,
           odd_w_out, ffn_norm, ffn_w_in, ffn_conv_w, ffn_conv_b, ffn_w_out, final_norm):
    batch, seq, d = x.shape
    t = batch * seq
    xr = x.reshape(t, d)
    tm = _tile(seq, 512)

    ffn_w_in_b = ffn_w_in.astype(BF16)
    ffn_w_out_b = ffn_w_out.astype(BF16)

    def mix_ffn(acts, ws, xr, i, final):
        cwb = _pad_rows(jnp.concatenate([ffn_conv_w[i], ffn_conv_b[i][None, :]], axis=0), 8)
        return _mix_ffn(acts, ws, xr, ffn_norm[i][None, :], ffn_w_in_b, cwb, ffn_w_out_b,
                        final_norm[None, :], i, seq, tm, 256, final)

    w0 = even_w_in[0]
    n_qkv = 3 * DN_QK
    w_main = jnp.concatenate([w0[:, :n_qkv], w0[:, n_qkv + 2 * DN_HEADS:]], axis=1).astype(BF16)
    w_ab = jnp.pad(w0[:, n_qkv:n_qkv + 2 * DN_HEADS], ((0, 0), (0, LANES - 2 * DN_HEADS)))
    w_ab_hi = w_ab.astype(BF16)
    w_ab_lo = (w_ab - w_ab_hi.astype(F32)).astype(BF16)
    g0 = even_norm_mix[0][None, :]
    half = DIFF_HD // 2
    inv = 1.0 / (ROPE_THETA ** (jnp.arange(0, DIFF_HD, 2, dtype=F32) / DIFF_HD))
    ang = jnp.arange(seq, dtype=F32)[:, None] * inv[None, :]
    cos_t = jnp.tile(jnp.cos(ang), (1, LANES // half))
    sign = jnp.tile(jnp.concatenate([-jnp.ones((half,), F32), jnp.ones((half,), F32)]), LANES // DIFF_HD)
    sin_t = jnp.tile(jnp.sin(ang), (1, LANES // half)) * sign[None, :]
    p0, ab, qa, qb, kr, vt = _in_proj0(xr, g0, w_main, (w_ab_hi, w_ab_lo), _pad_rows(even_dn_conv[0], 8),
                                       cos_t, sin_t, batch, seq, tm, 512)

    def lane_vec(v):
        return jnp.pad(v.astype(F32), (0, LANES - v.shape[0]))[None, :]

    o_a = _deltanet(p0, ab, lane_vec(jnp.exp(even_dn_A_log[0])), lane_vec(even_dn_dt_bias[0]),
                    even_dn_norm[0][None, :], batch, seq, _tile(seq, 512))

    lam_init = 0.8 - 0.6 * math.exp(-0.3 * 0)
    lp = even_diff_lambda[0].astype(F32)
    lam = jnp.exp(jnp.sum(lp[0] * lp[1])) - jnp.exp(jnp.sum(lp[2] * lp[3])) + lam_init
    tq = _tile(seq, 2048)
    lam_row = jnp.full((1, tq), lam, F32)
    gain_t = jnp.broadcast_to((even_diff_subln[0] * (1.0 - lam_init))[:, None], (LANES, tq))
    o_b = _diff_attn(qa, qb, kr, vt, lam_row, gain_t, batch, seq, tq)

    wo = even_w_out[0].astype(BF16)
    xr = mix_ffn([o_a, o_b], [wo[:DN_QK], wo[DN_QK:]], xr, 0, False)

    wq = odd_w_qkv[0]
    log2e = math.log2(math.e)
    wq = jnp.concatenate([wq[:, :D_MODEL] * (CA_HD ** -0.5 * log2e), wq[:, D_MODEL:]], axis=1).astype(BF16)
    qkv = _norm_matmul(xr, odd_norm_mix[0][None, :], wq, tm, 512)
    pad = CA_LEFT_CHUNKS * CHUNK
    qi = jnp.arange(BAND_SUB)[:, None]
    kj = jnp.arange(BAND_WIN)[None, :]
    in_band = jnp.logical_and(kj // CHUNK >= qi // CHUNK, kj // CHUNK <= qi // CHUNK + CA_LEFT_CHUNKS)
    m_len = BAND_WIN + BAND_SUB + 1
    rel = jnp.arange(m_len) - BAND_SUB - pad
    e = odd_rel_bias[0].astype(F32)[:, jnp.clip(rel, -REL_CLIP, REL_CLIP) + REL_CLIP]
    bias = jnp.tile(e, (1, BAND_SUB))[:, :BAND_SUB * (m_len - 1)].reshape(CA_HEADS, BAND_SUB, m_len - 1)
    bias = jnp.where(in_band[None], bias[:, :, BAND_SUB:BAND_SUB + BAND_WIN] * log2e, MASKED)
    o_c = _band_attn(qkv, bias, seq, tm)
    xr = mix_ffn([o_c], [odd_w_out[0].astype(BF16)], xr, 1, True)
    return xr.reshape(batch, seq, d)
```

```python
import functools
import math

import jax
import jax.numpy as jnp
from jax import lax
from jax.experimental import pallas as pl
from jax.experimental.pallas import tpu as pltpu

F32 = jnp.float32
BF16 = jnp.bfloat16

D_MODEL = 1024
CHUNK = 64
RMS_EPS = 1e-6
ROPE_THETA = 10000.0
DN_HEADS = 4
DN_DK = 128
DN_QK = 512
DN_CONV = 4
DIFF_HEADS = 4
DIFF_HD = 64
CA_HEADS = 16
CA_HD = 64
CA_LEFT_CHUNKS = 8
REL_CLIP = 256
D_FF = 2816
MASKED = -1e30
VMEM_LIMIT = 56 * 1024 * 1024
LANES = 128
BAND_SUB = 128
BAND_WIN = BAND_SUB + CA_LEFT_CHUNKS * CHUNK

def _params(*sem):
    return pltpu.CompilerParams(dimension_semantics=sem, vmem_limit_bytes=VMEM_LIMIT)


def _rms(x, g):
    ms = jnp.mean(x * x, axis=-1, keepdims=True)
    return x * lax.rsqrt(ms + RMS_EPS) * g


def _silu(x):
    return x * jax.nn.sigmoid(x)


def _dot(a, b):
    return jnp.dot(a, b, preferred_element_type=F32)


def _dot_nt(a, b):
    return lax.dot_general(a, b, (((1,), (1,)), ((), ())), preferred_element_type=F32)


def _dot_tn(a, b):
    return lax.dot_general(a, b, (((0,), (0,)), ((), ())), preferred_element_type=F32)


def _dot_exact_lhs(l_bf16, x):
    x1 = x.astype(BF16)
    x2 = (x - x1.astype(F32)).astype(BF16)
    return _dot(l_bf16, x1) + _dot(l_bf16, x2)


def _resident(shape):
    return pl.BlockSpec(shape, lambda *_: (0,) * len(shape), pipeline_mode=pl.Buffered(1))


def _norm_matmul_kernel(x_ref, g_ref, w_ref, o_ref, *, tn):
    xn = _rms(x_ref[...], g_ref[...]).astype(BF16)
    for n0 in range(0, w_ref.shape[1], tn):
        o_ref[:, n0:n0 + tn] = _dot(xn, w_ref[:, n0:n0 + tn]).astype(o_ref.dtype)


def _norm_matmul(x, g, w, tm, tn):
    t, d = x.shape
    n = w.shape[1]
    return pl.pallas_call(
        functools.partial(_norm_matmul_kernel, tn=tn),
        grid=(t // tm,),
        in_specs=[pl.BlockSpec((tm, d), lambda i: (i, 0)), _resident((1, d)), _resident(w.shape)],
        out_specs=pl.BlockSpec((tm, n), lambda i: (i, 0)),
        out_shape=jax.ShapeDtypeStruct((t, n), BF16),
        compiler_params=_params("parallel"),
        name="norm_matmul",
    )(x, g, w)


def _in_proj0_kernel(x_ref, g_ref, w_ref, whi_ref, wlo_ref, cw_ref, cos_ref, sin_ref,
                     dn_ref, ab_ref, qa_ref, qb_ref, kr_ref, vt_ref, xx_ref, *, tm, tn, tiles_per_batch):
    i = pl.program_id(0)
    n_conv = cw_ref.shape[1]

    @pl.when(i % tiles_per_batch == 0)
    def _():
        xx_ref[0:8, :] = jnp.zeros((8, n_conv), F32)

    xn = _rms(x_ref[...], g_ref[...])
    hi = xn.astype(BF16)
    lo_x = (xn - hi.astype(F32)).astype(BF16)
    ab_ref[...] = _dot(hi, whi_ref[...]) + (_dot(hi, wlo_ref[...]) + _dot(lo_x, whi_ref[...]))
    n_dn = dn_ref.shape[1]
    for n0 in range(0, n_dn, tn):
        cs = slice(n0, n0 + tn)
        if n0 < n_conv:
            xx_ref[8:, cs] = _dot(hi, w_ref[:, cs])
        else:
            dn_ref[:, cs] = _dot(hi, w_ref[:, cs]).astype(BF16)
    w = DIFF_HEADS * LANES
    fq = _dot(hi, w_ref[:, n_dn:n_dn + w])
    fk = _dot(hi, w_ref[:, n_dn + w:n_dn + 2 * w])
    fv = _dot(hi, w_ref[:, n_dn + 2 * w:n_dn + 3 * w])

    for n0 in range(0, n_conv, tn):
        cs = slice(n0, n0 + tn)
        cw = cw_ref[:, cs]
        y = cw[DN_CONV - 1:DN_CONV] * xx_ref[pl.ds(8, tm), cs]
        for s in range(1, DN_CONV):
            y = y + cw[DN_CONV - 1 - s:DN_CONV - s] * xx_ref[pl.ds(8 - s, tm), cs]
        dn_ref[:, cs] = _silu(y).astype(BF16)
    xx_ref[0:8, :] = xx_ref[tm:tm + 8, :]

    cos = cos_ref[...]
    sin = sin_ref[...]
    lane = lax.broadcasted_iota(jnp.int32, cos.shape, 1)
    first = (lane % DIFF_HD) < (DIFF_HD // 2)
    lo = lane < DIFF_HD

    def rope(x):
        rot = jnp.where(first, pltpu.roll(x, LANES - DIFF_HD // 2, 1), pltpu.roll(x, DIFF_HD // 2, 1))
        return x * cos + rot * sin

    for h in range(DIFF_HEADS):
        hs = slice(h * LANES, (h + 1) * LANES)
        q = rope(fq[:, hs]) * (DIFF_HD ** -0.5 * math.log2(math.e))
        qa_ref[:, hs] = jnp.where(lo, q, 0.0).astype(BF16)
        qb_ref[:, hs] = jnp.where(lo, 0.0, q).astype(BF16)
        kr_ref[:, hs] = rope(fk[:, hs]).astype(BF16)
        vt_ref[h] = fv[:, hs].T.astype(BF16)


def _in_proj0(x, g, w, w_hp, cwp, cos_t, sin_t, batch, seq, tm, tn):
    t, d = x.shape
    w_diff = DIFF_HEADS * LANES
    n_dn = w.shape[1] - 3 * w_diff
    spt = seq // tm
    row = lambda n: pl.BlockSpec((tm, n), lambda i: (i, 0))
    tab = pl.BlockSpec((tm, LANES), lambda i: (i % spt, 0))
    qkr = jax.ShapeDtypeStruct((t, w_diff), BF16)
    return pl.pallas_call(
        functools.partial(_in_proj0_kernel, tm=tm, tn=tn, tiles_per_batch=spt),
        grid=(t // tm,),
        in_specs=[row(d), _resident((1, d)), _resident(w.shape), _resident(w_hp[0].shape),
                  _resident(w_hp[1].shape), _resident(cwp.shape), tab, tab],
        out_specs=[row(n_dn), row(w_hp[0].shape[1]), row(w_diff), row(w_diff), row(w_diff),
                   pl.BlockSpec((None, DIFF_HEADS, None, LANES, tm), lambda i: (i // spt, 0, i % spt, 0, 0))],
        out_shape=[jax.ShapeDtypeStruct((t, n_dn), BF16), jax.ShapeDtypeStruct((t, w_hp[0].shape[1]), F32),
                   qkr, qkr, qkr, jax.ShapeDtypeStruct((batch, DIFF_HEADS, spt, LANES, tm), BF16)],
        scratch_shapes=[pltpu.VMEM((tm + 8, cwp.shape[1]), F32)],
        compiler_params=_params("arbitrary"),
        name="in_proj0",
    )(x, g, w, w_hp[0], w_hp[1], cwp, cos_t, sin_t)


def _mix_ffn_kernel(*refs, trans, tm, tf, tiles_per_batch, final_norm):
    n_in = len(trans)
    a_refs = refs[:n_in]
    w_refs = refs[n_in:2 * n_in]
    x_ref, g_ref, wi_ref, cw_ref, wo_ref, fg_ref, o_ref, gs_ref, act_ref = refs[2 * n_in:]
    i = pl.program_id(0)

    @pl.when(i % tiles_per_batch == 0)
    def _():
        gs_ref[0:8, :] = jnp.zeros((8, D_FF), F32)

    x = x_ref[...]
    for a_ref, w_ref, tr in zip(a_refs, w_refs, trans):
        x = x + (_dot_tn if tr else _dot)(a_ref[...], w_ref[...])
    xn = _rms(x, g_ref[...]).astype(BF16)
    for c0 in range(0, D_FF, tf):
        cs = slice(c0, c0 + tf)
        gate = _dot(xn, wi_ref[:, cs])
        up = _dot(xn, wi_ref[:, D_FF + c0:D_FF + c0 + tf])
        gs_ref[8:, cs] = gate
        cw = cw_ref[:, cs]
        y = (cw[2:3] * gate + cw[1:2] * gs_ref[pl.ds(7, tm), cs]
             + cw[0:1] * gs_ref[pl.ds(6, tm), cs] + cw[3:4])
        act_ref[:, cs] = (_silu(y) * up).astype(BF16)
    gs_ref[0:8, :] = gs_ref[tm:tm + 8, :]
    r = x + _dot(act_ref[...], wo_ref[...])
    o_ref[...] = _rms(r, fg_ref[...]) if final_norm else r


def _mix_ffn(acts, ws, x, g, w_in, cwb, w_out, final_g, layer, seq, tm, tf, final_norm):
    t, d = x.shape
    tpb = seq // tm
    trans = tuple(a.ndim == 3 for a in acts)
    kern = functools.partial(_mix_ffn_kernel, trans=trans, tm=tm, tf=tf, tiles_per_batch=tpb,
                             final_norm=final_norm)

    def of_layer(w):
        return pl.BlockSpec((None,) + w.shape[1:], lambda i: (layer, 0, 0), pipeline_mode=pl.Buffered(1))

    in_specs = ([pl.BlockSpec((None, a.shape[1], tm), lambda i: (i // tpb, 0, i % tpb)) if tr
                 else pl.BlockSpec((tm, a.shape[1]), lambda i: (i, 0)) for a, tr in zip(acts, trans)]
                + [_resident(w.shape) for w in ws]
                + [pl.BlockSpec((tm, d), lambda i: (i, 0)),
                   _resident((1, d)),
                   of_layer(w_in),
                   _resident(cwb.shape),
                   of_layer(w_out),
                   _resident((1, d))])
    return pl.pallas_call(
        kern,
        grid=(t // tm,),
        in_specs=in_specs,
        out_specs=pl.BlockSpec((tm, d), lambda i: (i, 0)),
        out_shape=jax.ShapeDtypeStruct((t, d), F32),
        scratch_shapes=[pltpu.VMEM((tm + 8, D_FF), F32),
                        pltpu.VMEM((tm, D_FF), BF16)],
        compiler_params=_params("arbitrary"),
        name="mix_ffn",
    )(*acts, *ws, x, g, w_in, cwb, w_out, final_g)


def _diff_attn_kernel(qa_ref, qb_ref, k_ref, vt_ref, lam_ref, gain_ref, o_ref, acc_ref, s_ref, *, tq, tk):
    qi = pl.program_id(2)
    acc_ref[...] = jnp.zeros(acc_ref.shape, F32)
    qs = (qa_ref[...], qb_ref[...])
    n_sub = tq // tk
    assert n_sub % 2 == 0

    def scores(ki, buf, q0=0):
        off = pl.multiple_of(ki * tk, tk)
        kt = k_ref[pl.ds(off, tk), :]
        for mp in range(2):
            s_ref[buf, mp, :, q0:] = _dot_nt(kt, qs[mp][q0:])

    def softmax_pv(ki, buf, carry, mask, q0=0):
        vt = vt_ref[ki]
        new, ps, alphas = [], [], []
        for mp in range(2):
            m_old, l_old = carry[2 * mp], carry[2 * mp + 1]
            s = s_ref[buf, mp, :, q0:]
            if mask is not None:
                s = jnp.where(mask[:, q0:], s, MASKED)
            m_new = jnp.maximum(m_old[:, q0:], jnp.max(s, axis=0, keepdims=True))
            alpha = jnp.exp2(m_old[:, q0:] - m_new)
            p = jnp.exp2(s - m_new)
            l_new = alpha * l_old[:, q0:] + jnp.sum(p, axis=0, keepdims=True)
            ps.append(p.astype(BF16))
            alphas.append(alpha)
            if q0:
                m_new = jnp.concatenate([m_old[:, :q0], m_new], axis=1)
                l_new = jnp.concatenate([l_old[:, :q0], l_new], axis=1)
            new += [m_new, l_new]
        for mp in range(2):
            acc_ref[mp, :, q0:] = alphas[mp] * acc_ref[mp, :, q0:] + _dot(vt, ps[mp])
        return tuple(new)

    row0 = jnp.full((1, tq), MASKED, F32)
    zero = jnp.zeros((1, tq), F32)

    scores(0, 0)

    def body(kj, c):
        scores(2 * kj + 1, 1)
        c = softmax_pv(2 * kj, 0, c, None)
        scores(2 * kj + 2, 0)
        return softmax_pv(2 * kj + 1, 1, c, None)

    carry = lax.fori_loop(0, qi * (n_sub // 2), body, (row0, zero, row0, zero))

    key = lax.broadcasted_iota(jnp.int32, (tk, tq), 0)
    qry = lax.broadcasted_iota(jnp.int32, (tk, tq), 1)
    for d in range(n_sub):
        if d + 1 < n_sub:
            scores(n_sub * qi + d + 1, (d + 1) % 2, (d + 1) * tk)
        carry = softmax_pv(n_sub * qi + d, d % 2, carry, (key // CHUNK + d * (tk // CHUNK)) <= (qry // CHUNK),
                           d * tk)
    _, l1, _, l2 = carry

    o = acc_ref[0] / l1 - lam_ref[...] * (acc_ref[1] / l2)
    ms = jnp.mean(o * o, axis=0, keepdims=True)
    o_ref[...] = (o * lax.rsqrt(ms + RMS_EPS) * gain_ref[...]).astype(o_ref.dtype)


def _diff_attn(qa, qb, kr, vt, lam_row, gain_t, batch, seq, tq):
    nq = seq // tq
    nk, _, tk = vt.shape[2:]
    kern = functools.partial(_diff_attn_kernel, tq=tq, tk=tk)
    qspec = pl.BlockSpec((tq, LANES), lambda b, h, q: (b * nq + q, h))
    return pl.pallas_call(
        kern,
        grid=(batch, DIFF_HEADS, nq),
        in_specs=[qspec, qspec,
                  pl.BlockSpec((seq, LANES), lambda b, h, q: (b, h)),
                  pl.BlockSpec((None, None, nk, LANES, tk), lambda b, h, q: (b, h, 0, 0, 0)),
                  pl.BlockSpec((1, tq), lambda b, h, q: (0, 0)),
                  pl.BlockSpec((LANES, tq), lambda b, h, q: (0, 0))],
        out_specs=pl.BlockSpec((None, LANES, tq), lambda b, h, q: (b, h, q)),
        out_shape=jax.ShapeDtypeStruct((batch, DIFF_HEADS * LANES, seq), BF16),
        scratch_shapes=[pltpu.VMEM((2, LANES, tq), F32),
                        pltpu.VMEM((2, 2, tk, tq), F32)],
        compiler_params=_params("parallel", "parallel", "arbitrary"),
        name="diff_attn",
    )(qa, qb, kr, vt, lam_row, gain_t)


def _band_attn_kernel(q_ref, kc_ref, kp_ref, vc_ref, vp_ref, bias_ref, o_ref,
                      kw_ref, vw_ref, *, tq, tiles_per_batch):
    i = pl.program_id(0)
    pad = CA_LEFT_CHUNKS * CHUNK
    kw_ref[0:tq, :] = kp_ref[...]
    kw_ref[tq:, :] = kc_ref[...]
    vw_ref[0:tq, :] = vp_ref[...]
    vw_ref[tq:, :] = vc_ref[...]
    has_prev = (i % tiles_per_batch) != 0
    lane = lax.broadcasted_iota(jnp.int32, (BAND_SUB, LANES), 1)
    lo = lane < CA_HD
    col = lax.broadcasted_iota(jnp.int32, (BAND_SUB, BAND_WIN), 1)

    def sub_tile(s, carry, first):
        r0 = pl.multiple_of(s * BAND_SUB, BAND_SUB)
        w0 = pl.multiple_of(r0 + (tq - pad), BAND_SUB)
        css = [slice(pr * LANES, (pr + 1) * LANES) for pr in range(CA_HEADS // 2)]
        scs = []
        for pr, cs in enumerate(css):
            qp = q_ref[pl.ds(r0, BAND_SUB), cs]
            kw = kw_ref[pl.ds(w0, BAND_WIN), cs]
            for hh in range(2):
                qh = jnp.where(lo, qp, 0) if hh == 0 else jnp.where(lo, 0, qp)
                scs.append(_dot_nt(qh.astype(BF16), kw))
        ps, dens = [], []
        for h, sc in enumerate(scs):
            sc = sc + bias_ref[h]
            if first:
                sc = jnp.where(col + w0 >= tq, sc, MASKED)
            p = jnp.exp2(sc - jnp.max(sc, axis=-1, keepdims=True))
            dens.append(jnp.sum(p, axis=-1, keepdims=True))
            ps.append(p.astype(BF16))
        for pr, cs in enumerate(css):
            vw = vw_ref[pl.ds(w0, BAND_WIN), cs]
            outs = [_dot(ps[2 * pr + hh], vw) / dens[2 * pr + hh] for hh in range(2)]
            o_ref[pl.ds(r0, BAND_SUB), cs] = jnp.where(lo, outs[0], outs[1]).astype(o_ref.dtype)
        return carry

    @pl.when(has_prev)
    def _():
        lax.fori_loop(0, tq // BAND_SUB, lambda s, c: sub_tile(s, c, False), 0)

    @pl.when(jnp.logical_not(has_prev))
    def _():
        lax.fori_loop(0, tq // BAND_SUB, lambda s, c: sub_tile(s, c, True), 0)


def _band_attn(qkv, bias, seq, tq):
    t = qkv.shape[0]
    d = CA_HEADS * CA_HD
    tpb = seq // tq
    kern = functools.partial(_band_attn_kernel, tq=tq, tiles_per_batch=tpb)

    def prev(i):
        return jnp.where(i % tpb == 0, i, i - 1)

    return pl.pallas_call(
        kern,
        grid=(t // tq,),
        in_specs=[pl.BlockSpec((tq, d), lambda i: (i, 0)),
                  pl.BlockSpec((tq, d), lambda i: (i, 1)),
                  pl.BlockSpec((tq, d), lambda i: (prev(i), 1)),
                  pl.BlockSpec((tq, d), lambda i: (i, 2)),
                  pl.BlockSpec((tq, d), lambda i: (prev(i), 2)),
                  _resident(bias.shape)],
        out_specs=pl.BlockSpec((tq, d), lambda i: (i, 0)),
        out_shape=jax.ShapeDtypeStruct((t, d), BF16),
        scratch_shapes=[pltpu.VMEM((2 * tq, d), BF16),
                        pltpu.VMEM((2 * tq, d), BF16)],
        compiler_params=_params("parallel"),
        name="band_attn",
    )(qkv, qkv, qkv, qkv, qkv, bias)


def _deltanet_kernel(q_ref, k_ref, v_ref, ab_ref, gate_ref, ea_ref, dtb_ref, ng_ref, o_ref, state_ref, *, tm):
    i = pl.program_id(1)
    pair = 2 * CHUNK

    @pl.when(i == 0)
    def _():
        state_ref[...] = jnp.zeros(state_ref.shape, F32)

    qc = q_ref[...].astype(F32)
    kc = k_ref[...].astype(F32)
    vc = v_ref[...].astype(F32)

    ab = ab_ref[...]
    g_all = -ea_ref[...] * jax.nn.softplus(ab + dtb_ref[...])
    beta_all = jax.nn.sigmoid(ab)

    row = lax.broadcasted_iota(jnp.int32, (pair, pair), 0)
    col = lax.broadcasted_iota(jnp.int32, (pair, pair), 1)
    same = (row // CHUNK) == (col // CHUNK)
    tri = jnp.logical_and(same, col <= row)
    strict = jnp.logical_and(same, col < row)
    lmat = jnp.where(tri, 1.0, 0.0).astype(BF16)
    su = jnp.where(jnp.logical_and(same, row > col), 1.0, 0.0)
    eye = jnp.where(row == col, 1.0, 0.0)
    top = row < CHUNK

    npair = tm // pair
    units = [(h, dc) for h in range(DN_HEADS) for dc in range(npair)]
    nu = len(units)
    hsl = [slice(h * DN_DK, (h + 1) * DN_DK) for h in range(DN_HEADS)]
    rsl = [slice(dc * pair, (dc + 1) * pair) for dc in range(npair)]

    def bf(x):
        return x.astype(BF16)

    qn, kn, g_b, beta_b = [], [], [], []
    for h in range(DN_HEADS):
        qh = qc[:, hsl[h]]
        kh = kc[:, hsl[h]]
        qn.append(qh * lax.rsqrt(jnp.sum(qh * qh, axis=-1, keepdims=True) + RMS_EPS) * (DN_DK ** -0.5))
        kn.append(kh * lax.rsqrt(jnp.sum(kh * kh, axis=-1, keepdims=True) + RMS_EPS))
        g_b.append(jnp.broadcast_to(g_all[:, h:h + 1], (tm, pair)))
        beta_b.append(jnp.broadcast_to(beta_all[:, DN_HEADS + h:DN_HEADS + h + 1], (tm, pair)))
    q2 = [qn[h][rsl[dc]] for h, dc in units]
    k2 = [kn[h][rsl[dc]] for h, dc in units]
    v2 = [vc[rsl[dc], hsl[h]] for h, dc in units]
    b2 = [beta_b[h][rsl[dc]] for h, dc in units]
    k2b = [bf(x) for x in k2]

    g_cat = jnp.concatenate([g_b[h][rsl[dc]] for h, dc in units], axis=1)
    gcum_cat = _dot_exact_lhs(lmat, g_cat)
    gdiff_cat = _dot_exact_lhs(lmat, g_cat * jnp.concatenate([su] * nu, axis=1))
    usl = [slice(u * pair, (u + 1) * pair) for u in range(nu)]
    gcum = [gcum_cat[:, s] for s in usl]
    decay = [jnp.where(tri, jnp.exp(jnp.where(tri, gdiff_cat[:, s], 0.0)), 0.0) for s in usl]
    kk = [_dot_nt(x, x) for x in k2b]
    a = [jnp.where(strict, b2[u] * kk[u] * decay[u], 0.0) for u in range(nu)]
    tinv = [eye - x for x in a]
    pw = a
    for _ in range(5):
        pwb = [bf(x) for x in pw]
        pw = [_dot(x, x) for x in pwb]
        pwb = [bf(x) for x in pw]
        tinv = [tinv[u] + _dot(bf(tinv[u]), pwb[u]) for u in range(nu)]
    eg = [jnp.exp(x) for x in gcum]
    rhs = [jnp.concatenate([v2[u] * b2[u], k2[u] * b2[u] * eg[u]], axis=1) for u in range(nu)]
    uw = [_dot(bf(tinv[u]), bf(rhs[u])) for u in range(nu)]
    qk = [jnp.where(tri, _dot_nt(bf(q2[u]), k2b[u]) * decay[u], 0.0) for u in range(nu)]
    qd = [bf(q2[u] * eg[u]) for u in range(nu)]
    gl = [(gcum[u][CHUNK - 1:CHUNK], gcum[u][pair - 1:pair]) for u in range(nu)]
    kd = [bf(k2[u] * jnp.exp(jnp.where(top, gl[u][0], gl[u][1]) - gcum[u])) for u in range(nu)]
    egl = [(jnp.exp(gl[u][0]), jnp.exp(gl[u][1])) for u in range(nu)]

    st = [state_ref[h] for h in range(DN_HEADS)]
    vns = [[None, None] for _ in range(nu)]
    ois = [[None, None] for _ in range(nu)]
    for dc in range(npair):
        for c in range(2):
            cr = slice(c * CHUNK, (c + 1) * CHUNK)
            us = [h * npair + dc for h in range(DN_HEADS)]
            stb = [bf(x) for x in st]
            vn = [uw[u][cr, :DN_DK] - _dot(bf(uw[u][cr, DN_DK:]), stb[h]) for h, u in enumerate(us)]
            for h, u in enumerate(us):
                ois[u][c] = _dot(qd[u][cr], stb[h])
                vns[u][c] = vn[h]
            st = [st[h] * egl[u][c] + _dot_tn(kd[u][cr], bf(vn[h])) for h, u in enumerate(us)]
    for h in range(DN_HEADS):
        state_ref[h] = st[h]

    for u, (h, dc) in enumerate(units):
        vn2 = jnp.concatenate(vns[u], axis=0)
        o = jnp.concatenate(ois[u], axis=0) + _dot(bf(qk[u]), bf(vn2))
        o = _rms(o, ng_ref[...]) * _silu(gate_ref[rsl[dc], hsl[h]].astype(F32))
        o_ref[rsl[dc], hsl[h]] = o.astype(o_ref.dtype)


def _deltanet(p0, ab, ea, dtb, ng, batch, seq, tm):
    t = p0.shape[0]
    w = DN_QK
    nt = seq // tm
    kern = functools.partial(_deltanet_kernel, tm=tm)

    def cur(cb):
        return pl.BlockSpec((tm, w), lambda b, i: (b * nt + i, cb))

    vec = pl.BlockSpec((1, LANES), lambda b, i: (0, 0))
    return pl.pallas_call(
        kern,
        grid=(batch, nt),
        in_specs=[cur(0), cur(1), cur(2),
                  pl.BlockSpec((tm, LANES), lambda b, i: (b * nt + i, 0)),
                  cur(3), vec, vec, vec],
        out_specs=pl.BlockSpec((tm, w), lambda b, i: (b * nt + i, 0)),
        out_shape=jax.ShapeDtypeStruct((t, w), BF16),
        scratch_shapes=[pltpu.VMEM((DN_HEADS, DN_DK, DN_DK), F32)],
        compiler_params=_params("arbitrary", "arbitrary"),
        name="deltanet",
    )(p0, p0, p0, ab, p0, ea, dtb, ng)


def _tile(n, want):
    return min(n, want)


def _pad_rows(a, rows):
    return jnp.concatenate([a, jnp.zeros((rows - a.shape[0], a.shape[1]), a.dtype)], axis=0)


def kernel(x, even_norm_mix, even_w_in, even_dn_conv, even_dn_A_log, even_dn_dt_bias, even_dn_norm,
           even_diff_lambda, even_diff_subln, even_w_out, odd_norm_mix, odd_w_qkv, odd_rel_bias,
           odd_w_out, ffn_norm, ffn_w_in, ffn_conv_w, ffn_conv_b, ffn_w_out, final_norm):
    batch, seq, d = x.shape
    t = batch * seq
    xr = x.reshape(t, d)
    tm = _tile(seq, 512)

    ffn_w_in_b = ffn_w_in.astype(BF16)
    ffn_w_out_b = ffn_w_out.astype(BF16)

    def mix_ffn(acts, ws, xr, i, final):
        cwb = _pad_rows(jnp.concatenate([ffn_conv_w[i], ffn_conv_b[i][None, :]], axis=0), 8)
        return _mix_ffn(acts, ws, xr, ffn_norm[i][None, :], ffn_w_in_b, cwb, ffn_w_out_b,
                        final_norm[None, :], i, seq, tm, 256, final)

    w0 = even_w_in[0]
    n_qkv = 3 * DN_QK
    w_main = jnp.concatenate([w0[:, :n_qkv], w0[:, n_qkv + 2 * DN_HEADS:]], axis=1).astype(BF16)
    w_ab = jnp.pad(w0[:, n_qkv:n_qkv + 2 * DN_HEADS], ((0, 0), (0, LANES - 2 * DN_HEADS)))
    w_ab_hi = w_ab.astype(BF16)
    w_ab_lo = (w_ab - w_ab_hi.astype(F32)).astype(BF16)
    g0 = even_norm_mix[0][None, :]
    half = DIFF_HD // 2
    inv = 1.0 / (ROPE_THETA ** (jnp.arange(0, DIFF_HD, 2, dtype=F32) / DIFF_HD))
    ang = jnp.arange(seq, dtype=F32)[:, None] * inv[None, :]
    cos_t = jnp.tile(jnp.cos(ang), (1, LANES // half))
    sign = jnp.tile(jnp.concatenate([-jnp.ones((half,), F32), jnp.ones((half,), F32)]), LANES // DIFF_HD)
    sin_t = jnp.tile(jnp.sin(ang), (1, LANES // half)) * sign[None, :]
    p0, ab, qa, qb, kr, vt = _in_proj0(xr, g0, w_main, (w_ab_hi, w_ab_lo), _pad_rows(even_dn_conv[0], 8),
                                       cos_t, sin_t, batch, seq, tm, 512)

    def lane_vec(v):
        return jnp.pad(v.astype(F32), (0, LANES - v.shape[0]))[None, :]

    o_a = _deltanet(p0, ab, lane_vec(jnp.exp(even_dn_A_log[0])), lane_vec(even_dn_dt_bias[0]),
                    even_dn_norm[0][None, :], batch, seq, _tile(seq, 512))

    lam_init = 0.8 - 0.6 * math.exp(-0.3 * 0)
    lp = even_diff_lambda[0].astype(F32)
    lam = jnp.exp(jnp.sum(lp[0] * lp[1])) - jnp.exp(jnp.sum(lp[2] * lp[3])) + lam_init
    tq = _tile(seq, 2048)
    lam_row = jnp.full((1, tq), lam, F32)
    gain_t = jnp.broadcast_to((even_diff_subln[0] * (1.0 - lam_init))[:, None], (LANES, tq))
    o_b = _diff_attn(qa, qb, kr, vt, lam_row, gain_t, batch, seq, tq)

    wo = even_w_out[0].astype(BF16)
    xr = mix_ffn([o_a, o_b], [wo[:DN_QK], wo[DN_QK:]], xr, 0, False)

    wq = odd_w_qkv[0]
    log2e = math.log2(math.e)
    wq = jnp.concatenate([wq[:, :D_MODEL] * (CA_HD ** -0.5 * log2e), wq[:, D_MODEL:]], axis=1).astype(BF16)
    qkv = _norm_matmul(xr, odd_norm_mix[0][None, :], wq, tm, 512)
    pad = CA_LEFT_CHUNKS * CHUNK
    qi = jnp.arange(BAND_SUB)[:, None]
    kj = jnp.arange(BAND_WIN)[None, :]
    in_band = jnp.logical_and(kj // CHUNK >= qi // CHUNK, kj // CHUNK <= qi // CHUNK + CA_LEFT_CHUNKS)
    m_len = BAND_WIN + BAND_SUB + 1
    rel = jnp.arange(m_len) - BAND_SUB - pad
    e = odd_rel_bias[0].astype(F32)[:, jnp.clip(rel, -REL_CLIP, REL_CLIP) + REL_CLIP]
    bias = jnp.tile(e, (1, BAND_SUB))[:, :BAND_SUB * (m_len - 1)].reshape(CA_HEADS, BAND_SUB, m_len - 1)
    bias = jnp.where(in_band[None], bias[:, :, BAND_SUB:BAND_SUB + BAND_WIN] * log2e, MASKED)
    o_c = _band_attn(qkv, bias, seq, _tile(seq, 1024))
    xr = mix_ffn([o_c], [odd_w_out[0].astype(BF16)], xr, 1, True)
    return xr.reshape(batch, seq, d)
```
